```python
import jax, jax.numpy as jnp
from jax import lax
import numpy as np

D_MODEL = 2048
BATCH = 4
SEQ = 8192
DEPTH = 1
DEC_BATCH = 8
DEC_SEQ = 32
PAST_LEN = 1024

CHUNK = 64
Q_BLOCK = 128
EPS = 1e-6
N_HEADS = 16
QK_NOPE = 128
QK_ROPE = 64
V_HEAD = 128
Q_LORA = 512
KV_LORA = 512
ROPE_THETA = 10000.0
MLA_SCALE = (QK_NOPE + QK_ROPE) ** -0.5
CONV_DIM = 1024
CONV_WIDTH = 31
MEM_LEN = 256
MEM_HEADS = 4
MEM_HEAD_DIM = 256
MEM_DIM = MEM_HEADS * MEM_HEAD_DIM
MEM_SCALE = MEM_HEAD_DIM ** -0.5
N_BRANCH = 3
D_FF = 5632
FFN_CONV_WIDTH = 3
OFF_CKV = Q_LORA
OFF_KR = OFF_CKV + KV_LORA
OFF_GLU = OFF_KR + QK_ROPE
OFF_QM = OFF_GLU + 2 * CONV_DIM
OFF_GATE = OFF_QM + MEM_DIM
IN_DIM = OFF_GATE + N_BRANCH * D_MODEL

kernel_name = 'hybrid_mla_conformer_memory_stream_step'


def rmsnorm(x, g):
    xf = x.astype(jnp.float32)
    y = xf * lax.rsqrt(jnp.mean(xf * xf, axis=-1, keepdims=True) + EPS)
    return (y * g.astype(jnp.float32)).astype(x.dtype)


def layernorm(x, g, b):
    xf = x.astype(jnp.float32)
    mu = jnp.mean(xf, axis=-1, keepdims=True)
    var = jnp.mean(jnp.square(xf - mu), axis=-1, keepdims=True)
    y = (xf - mu) * lax.rsqrt(var + EPS)
    return (y * g.astype(jnp.float32) + b.astype(jnp.float32)).astype(x.dtype)


def apply_rope(x, pos):
    half = QK_ROPE // 2
    inv_freq = ROPE_THETA ** (-jnp.arange(half, dtype=jnp.float32) / half)
    ang = pos.astype(jnp.float32)[:, None] * inv_freq[None, :]
    cos = jnp.cos(ang)[None, :, None, :]
    sin = jnp.sin(ang)[None, :, None, :]
    xf = x.astype(jnp.float32)
    x1, x2 = xf[..., :half], xf[..., half:]
    return jnp.concatenate([x1 * cos - x2 * sin, x1 * sin + x2 * cos], axis=-1).astype(x.dtype)


def causal_dwconv(u, hist, w, b):
    full = jnp.concatenate([hist, u], axis=1)
    y = lax.conv_general_dilated(full, w[:, None, :], window_strides=(1,), padding='VALID',
                                 dimension_numbers=('NWC', 'WIO', 'NWC'),
                                 feature_group_count=u.shape[-1])
    return y + b, full[:, full.shape[1] - (w.shape[0] - 1):]


def mla_attend(q_nope, q_rope, k_nope, k_rope, v, q_chunk, k_chunk):
    s = (jnp.einsum('bqhd,bkhd->bhqk', q_nope, k_nope)
         + jnp.einsum('bqhr,bkr->bhqk', q_rope, k_rope)).astype(jnp.float32) * MLA_SCALE
    mask = k_chunk[None, :] <= q_chunk[:, None]
    p = jax.nn.softmax(jnp.where(mask, s, -jnp.inf), axis=-1).astype(v.dtype)
    return jnp.einsum('bhqk,bkhd->bqhd', p, v)


def memory_kv(mem, g, wk, wv):
    b, m, _ = mem.shape
    mn = rmsnorm(mem, g)
    k = (mn @ wk).reshape(b, m, MEM_HEADS, MEM_HEAD_DIM)
    v = (mn @ wv).reshape(b, m, MEM_HEADS, MEM_HEAD_DIM)
    return k, v


def encoder_layer(x, pos, hist_conv, hist_ffn, past_ckv, past_krope, mem_k, mem_v, p):
    B, T, _ = x.shape
    P = past_ckv.shape[1]
    xn = rmsnorm(x, p['g_mix'])
    z = xn @ p['w_in']
    c_q = z[..., :OFF_CKV]
    c_kv = z[..., OFF_CKV:OFF_KR]
    k_rope = z[..., OFF_KR:OFF_GLU]
    u_glu = z[..., OFF_GLU:OFF_QM]
    q_mem = z[..., OFF_QM:OFF_GATE]
    gate_logits = z[..., OFF_GATE:]

    c_q = rmsnorm(c_q, p['g_cq'])
    q = jnp.einsum('btc,chd->bthd', c_q, p['w_uq'])
    q_nope = q[..., :QK_NOPE]
    q_rope = apply_rope(q[..., QK_NOPE:], pos)
    c_kv = rmsnorm(c_kv, p['g_ckv'])
    k_rope = apply_rope(k_rope[:, :, None, :], pos)[:, :, 0, :]
    ckv_all = jnp.concatenate([past_ckv, c_kv], axis=1)
    kr_all = jnp.concatenate([past_krope, k_rope], axis=1)
    kv = jnp.einsum('bsc,chd->bshd', ckv_all, p['w_ukv'])
    k_nope, v = kv[..., :QK_NOPE], kv[..., QK_NOPE:]
    k_chunk = jnp.arange(P + T) // CHUNK
    q_chunk = pos // CHUNK
    qb = Q_BLOCK if T % Q_BLOCK == 0 else T
    outs = []
    for i in range(T // qb):
        q0, q1 = i * qb, (i + 1) * qb
        kend = P + q1
        outs.append(mla_attend(q_nope[:, q0:q1], q_rope[:, q0:q1], k_nope[:, :kend], kr_all[:, :kend],
                               v[:, :kend], q_chunk[q0:q1], k_chunk[:kend]))
    attn = jnp.concatenate(outs, axis=1).reshape(B, T, N_HEADS * V_HEAD)
    a_out = attn @ p['w_mla_o']

    glu = u_glu[..., :CONV_DIM] * jax.nn.sigmoid(u_glu[..., CONV_DIM:])
    cv, new_hist_conv = causal_dwconv(glu, hist_conv, p['w_conv_dw'], p['b_conv_dw'])
    cv = jax.nn.silu(layernorm(cv, p['ln_conv_g'], p['ln_conv_b']))
    b_out = cv @ p['w_conv_o']

    qm = q_mem.reshape(B, T, MEM_HEADS, MEM_HEAD_DIM)
    sm = jnp.einsum('bthd,bmhd->bhtm', qm, mem_k).astype(jnp.float32) * MEM_SCALE
    pm = jax.nn.softmax(sm, axis=-1).astype(mem_v.dtype)
    om = jnp.einsum('bhtm,bmhd->bthd', pm, mem_v).reshape(B, T, MEM_DIM)
    c_out = om @ p['w_mem_o']

    gates = jax.nn.sigmoid(gate_logits + p['b_gate']).reshape(B, T, N_BRANCH, D_MODEL)
    mix = gates[:, :, 0] * a_out + gates[:, :, 1] * b_out + gates[:, :, 2] * c_out
    h = x + mix @ p['w_out']

    hn = rmsnorm(h, p['g_ffn'])
    up = hn @ p['w_up']
    a, val = up[..., :D_FF], up[..., D_FF:]
    a, new_hist_ffn = causal_dwconv(a, hist_ffn, p['w_ffn_dw'], p['b_ffn_dw'])
    h = h + (jax.nn.silu(a) * val) @ p['w_down']
    return h, c_kv, k_rope, new_hist_conv, new_hist_ffn


def setup_inputs(seed: int = 0) -> dict:
    key = jax.random.key(seed)
    ks = jax.random.split(key, 40)
    f32 = jnp.float32

    def nrm(k, shape, scale):
        return jax.random.normal(k, shape, f32) * scale

    def gain(k, shape):
        return 1.0 + 0.01 * jax.random.normal(k, shape, f32)

    L = DEPTH
    return {
        'x_prompt': nrm(ks[0], (BATCH, SEQ, D_MODEL), 1.0),
        'x_sample': nrm(ks[1], (DEC_BATCH, DEC_SEQ, D_MODEL), 1.0),
        'mem_prompt': nrm(ks[2], (BATCH, MEM_LEN, D_MODEL), 1.0),
        'cache_ckv': nrm(ks[3], (L, DEC_BATCH, PAST_LEN, KV_LORA), 1.0),
        'cache_krope': nrm(ks[4], (L, DEC_BATCH, PAST_LEN, QK_ROPE), 1.0),
        'state_conv': nrm(ks[5], (L, DEC_BATCH, CONV_WIDTH - 1, CONV_DIM), 0.5),
        'state_ffn_conv': nrm(ks[6], (L, DEC_BATCH, FFN_CONV_WIDTH - 1, D_FF), 1.0),
        'cache_mem_k': nrm(ks[7], (L, DEC_BATCH, MEM_LEN, MEM_HEADS, MEM_HEAD_DIM), 1.0),
        'cache_mem_v': nrm(ks[8], (L, DEC_BATCH, MEM_LEN, MEM_HEADS, MEM_HEAD_DIM), 1.0),
        'g_mix': gain(ks[9], (L, D_MODEL)),
        'w_in': nrm(ks[10], (L, D_MODEL, IN_DIM), D_MODEL ** -0.5),
        'g_cq': gain(ks[11], (L, Q_LORA)),
        'w_uq': nrm(ks[12], (L, Q_LORA, N_HEADS, QK_NOPE + QK_ROPE), Q_LORA ** -0.5),
        'g_ckv': gain(ks[13], (L, KV_LORA)),
        'w_ukv': nrm(ks[14], (L, KV_LORA, N_HEADS, QK_NOPE + V_HEAD), KV_LORA ** -0.5),
        'w_mla_o': nrm(ks[15], (L, N_HEADS * V_HEAD, D_MODEL), (N_HEADS * V_HEAD) ** -0.5),
        'w_conv_dw': nrm(ks[16], (L, CONV_WIDTH, CONV_DIM), CONV_WIDTH ** -0.5),
        'b_conv_dw': nrm(ks[17], (L, CONV_DIM), 0.01),
        'ln_conv_g': gain(ks[18], (L, CONV_DIM)),
        'ln_conv_b': nrm(ks[19], (L, CONV_DIM), 0.01),
        'w_conv_o': nrm(ks[20], (L, CONV_DIM, D_MODEL), CONV_DIM ** -0.5),
        'g_mem': gain(ks[21], (L, D_MODEL)),
        'w_mem_k': nrm(ks[22], (L, D_MODEL, MEM_DIM), D_MODEL ** -0.5),
        'w_mem_v': nrm(ks[23], (L, D_MODEL, MEM_DIM), D_MODEL ** -0.5),
        'w_mem_o': nrm(ks[24], (L, MEM_DIM, D_MODEL), MEM_DIM ** -0.5),
        'b_gate': nrm(ks[25], (L, N_BRANCH * D_MODEL), 0.01),
        'w_out': nrm(ks[26], (L, D_MODEL, D_MODEL), D_MODEL ** -0.5),
        'g_ffn': gain(ks[27], (L, D_MODEL)),
        'w_up': nrm(ks[28], (L, D_MODEL, 2 * D_FF), D_MODEL ** -0.5),
        'w_ffn_dw': nrm(ks[29], (L, FFN_CONV_WIDTH, D_FF), FFN_CONV_WIDTH ** -0.5),
        'b_ffn_dw': nrm(ks[30], (L, D_FF), 0.01),
        'w_down': nrm(ks[31], (L, D_FF, D_MODEL), D_FF ** -0.5),
        'g_final': gain(ks[32], (D_MODEL,)),
    }


def reference(x_prompt, x_sample, mem_prompt, cache_ckv, cache_krope, state_conv, state_ffn_conv,
              cache_mem_k, cache_mem_v, g_mix, w_in, g_cq, w_uq, g_ckv, w_ukv, w_mla_o,
              w_conv_dw, b_conv_dw, ln_conv_g, ln_conv_b, w_conv_o, g_mem, w_mem_k, w_mem_v,
              w_mem_o, b_gate, w_out, g_ffn, w_up, w_ffn_dw, b_ffn_dw, w_down, g_final):
    B, T = x_prompt.shape[0], x_prompt.shape[1]
    Td = x_sample.shape[1]
    P = cache_ckv.shape[2]
    dt = x_prompt.dtype
    pos_p = jnp.arange(T)
    pos_s = P + jnp.arange(Td)
    hp, hs = x_prompt, x_sample
    ckv_p_l, kr_p_l, cs_p_l, fs_p_l, mk_p_l, mv_p_l = [], [], [], [], [], []
    ckv_s_l, kr_s_l, cs_s_l, fs_s_l = [], [], [], []
    for l in range(DEPTH):
        p = {'g_mix': g_mix[l], 'w_in': w_in[l], 'g_cq': g_cq[l], 'w_uq': w_uq[l], 'g_ckv': g_ckv[l],
             'w_ukv': w_ukv[l], 'w_mla_o': w_mla_o[l], 'w_conv_dw': w_conv_dw[l], 'b_conv_dw': b_conv_dw[l],
             'ln_conv_g': ln_conv_g[l], 'ln_conv_b': ln_conv_b[l], 'w_conv_o': w_conv_o[l],
             'w_mem_o': w_mem_o[l], 'b_gate': b_gate[l], 'w_out': w_out[l], 'g_ffn': g_ffn[l],
             'w_up': w_up[l], 'w_ffn_dw': w_ffn_dw[l], 'b_ffn_dw': b_ffn_dw[l], 'w_down': w_down[l]}
        mk_p, mv_p = memory_kv(mem_prompt, g_mem[l], w_mem_k[l], w_mem_v[l])
        hp, ckv_p, kr_p, cs_p, fs_p = encoder_layer(
            hp, pos_p,
            jnp.zeros((B, CONV_WIDTH - 1, CONV_DIM), dt),
            jnp.zeros((B, FFN_CONV_WIDTH - 1, D_FF), dt),
            jnp.zeros((B, 0, KV_LORA), dt), jnp.zeros((B, 0, QK_ROPE), dt),
            mk_p, mv_p, p)
        hs, ckv_s, kr_s, cs_s, fs_s = encoder_layer(
            hs, pos_s, state_conv[l], state_ffn_conv[l], cache_ckv[l], cache_krope[l],
            cache_mem_k[l], cache_mem_v[l], p)
        ckv_p_l.append(ckv_p); kr_p_l.append(kr_p); cs_p_l.append(cs_p); fs_p_l.append(fs_p)
        mk_p_l.append(mk_p); mv_p_l.append(mv_p)
        ckv_s_l.append(ckv_s); kr_s_l.append(kr_s); cs_s_l.append(cs_s); fs_s_l.append(fs_s)
    y_prompt = rmsnorm(hp, g_final)
    y_sample = rmsnorm(hs, g_final)
    ckv_prompt = jnp.stack(ckv_p_l)
    krope_prompt = jnp.stack(kr_p_l)
    conv_prompt = jnp.stack(cs_p_l)
    ffn_conv_prompt = jnp.stack(fs_p_l)
    mem_k_prompt = jnp.stack(mk_p_l)
    mem_v_prompt = jnp.stack(mv_p_l)
    ckv_sample = jnp.stack(ckv_s_l)
    krope_sample = jnp.stack(kr_s_l)
    conv_sample = jnp.stack(cs_s_l)
    ffn_conv_sample = jnp.stack(fs_s_l)
    return (y_prompt, y_sample, ckv_prompt, krope_prompt, conv_prompt, ffn_conv_prompt,
            mem_k_prompt, mem_v_prompt, ckv_sample, krope_sample, conv_sample, ffn_conv_sample)
```

```python
import functools

import numpy as np
import jax
import jax.numpy as jnp
from jax import lax
from jax.experimental import pallas as pl
from jax.experimental.pallas import tpu as pltpu

D_MODEL = 2048
CHUNK = 64
EPS = 1e-6
N_HEADS = 16
QK_NOPE = 128
QK_ROPE = 64
V_HEAD = 128
Q_LORA = 512
KV_LORA = 512
ROPE_THETA = 10000.0
MLA_SCALE = (QK_NOPE + QK_ROPE) ** -0.5
CONV_DIM = 1024
CONV_WIDTH = 31
MEM_LEN = 256
MEM_HEADS = 4
MEM_HEAD_DIM = 256
MEM_DIM = MEM_HEADS * MEM_HEAD_DIM
MEM_SCALE = MEM_HEAD_DIM ** -0.5
N_BRANCH = 3
D_FF = 5632
FFN_CONV_WIDTH = 3
OFF_CKV = Q_LORA
OFF_KR = OFF_CKV + KV_LORA
OFF_GLU = OFF_KR + QK_ROPE
OFF_QM = OFF_GLU + 2 * CONV_DIM
OFF_GATE = OFF_QM + MEM_DIM

LANES = 128
V7X_VMEM_LIMIT = 56 * 1024 * 1024
Q_HEAD_W = 2 * LANES
NEG_BIG = -1e30

BF16 = jnp.bfloat16
F32 = jnp.float32


def _cparams(sem, vmem=V7X_VMEM_LIMIT):
    return pltpu.CompilerParams(dimension_semantics=sem, vmem_limit_bytes=vmem)


def _row_tile(n, cap, mult=16):
    t = min(cap, n)
    t -= t % mult
    while t > mult and n % t:
        t -= mult
    assert t >= mult and n % t == 0, (n, cap)
    return t


def _rms(x, g):
    y = x * lax.rsqrt(jnp.mean(x * x, axis=-1, keepdims=True) + EPS)
    return y * g


def _dot(a, b):
    return jnp.dot(a, b, preferred_element_type=F32)


def _dot_nt(a, b):
    return lax.dot_general(a, b, (((1,), (1,)), ((), ())), preferred_element_type=F32)


def _proj_lat_kernel(x_ref, gmix_ref, w_ref, gcq_ref, gckv_ref, cos_ref, sin_ref,
                     xn_ref, cqn_ref, ckv_ref, ckvb_ref, kr_ref, krpad_ref):
    xn = _rms(x_ref[...], gmix_ref[...]).astype(BF16)
    xn_ref[...] = xn
    z = _dot(xn, w_ref[...])
    cqn_ref[...] = _rms(z[:, :Q_LORA], gcq_ref[...]).astype(BF16)
    ckv = _rms(z[:, OFF_CKV:OFF_KR], gckv_ref[...])
    ckv_ref[...] = ckv
    ckvb_ref[...] = ckv.astype(BF16)
    r = z[:, OFF_KR:OFF_KR + LANES] * cos_ref[...] + z[:, OFF_KR + LANES:] * sin_ref[...]
    kr_ref[...] = r[:, :QK_ROPE]
    lane = lax.broadcasted_iota(jnp.int32, r.shape, 1)
    zero = jnp.zeros_like(r)
    krpad_ref[:, :LANES] = jnp.where(lane < QK_ROPE, r, zero).astype(BF16)
    krpad_ref[:, LANES:] = jnp.where(lane >= QK_ROPE, r, zero).astype(BF16)


def _proj_lat(x, g_mix, w1, g_cq, g_ckv, cos_t, sin_t, tm):
    n = x.shape[0]
    nt = cos_t.shape[0] // tm
    row = lambda i: (i, 0)
    const = lambda i: (0, 0)
    tab = lambda i: (i % nt, 0)
    return pl.pallas_call(
        _proj_lat_kernel,
        grid=(n // tm,),
        in_specs=[
            pl.BlockSpec((tm, D_MODEL), row),
            pl.BlockSpec((1, D_MODEL), const),
            pl.BlockSpec(w1.shape, const),
            pl.BlockSpec((1, Q_LORA), const),
            pl.BlockSpec((1, KV_LORA), const),
            pl.BlockSpec((tm, LANES), tab),
            pl.BlockSpec((tm, LANES), tab),
        ],
        out_specs=[
            pl.BlockSpec((tm, D_MODEL), row),
            pl.BlockSpec((tm, Q_LORA), row),
            pl.BlockSpec((tm, KV_LORA), row),
            pl.BlockSpec((tm, KV_LORA), row),
            pl.BlockSpec((tm, QK_ROPE), row),
            pl.BlockSpec((tm, Q_HEAD_W), row),
        ],
        out_shape=[
            jax.ShapeDtypeStruct((n, D_MODEL), BF16),
            jax.ShapeDtypeStruct((n, Q_LORA), BF16),
            jax.ShapeDtypeStruct((n, KV_LORA), F32),
            jax.ShapeDtypeStruct((n, KV_LORA), BF16),
            jax.ShapeDtypeStruct((n, QK_ROPE), F32),
            jax.ShapeDtypeStruct((n, Q_HEAD_W), BF16),
        ],
        compiler_params=_cparams(("parallel",)),
        name="proj_lat",
    )(x, g_mix, w1, g_cq, g_ckv, cos_t, sin_t)


def _proj_q_kernel(cqn_ref, w_ref, cos_ref, sin_ref, q_ref):
    cqn = cqn_ref[...]
    cos = cos_ref[...]
    sin = sin_ref[...]
    pw = 4 * LANES
    for p in range(N_HEADS // 2):
        r = _dot(cqn, w_ref[:, p * pw:(p + 1) * pw])
        rp = ((r[:, 2 * LANES:3 * LANES] * cos + r[:, 3 * LANES:] * sin) * MLA_SCALE).astype(BF16)
        q_ref[:, p * pw:p * pw + LANES] = (r[:, :LANES] * MLA_SCALE).astype(BF16)
        q_ref[:, p * pw + LANES:p * pw + 2 * LANES] = rp
        q_ref[:, p * pw + 2 * LANES:p * pw + 3 * LANES] = (r[:, LANES:2 * LANES] * MLA_SCALE).astype(BF16)
        q_ref[:, p * pw + 3 * LANES:(p + 1) * pw] = rp


def _proj_q(cqn, w2, cos_t, sin_t, tm):
    n = cqn.shape[0]
    nt = cos_t.shape[0] // tm
    return pl.pallas_call(
        _proj_q_kernel,
        grid=(n // tm,),
        in_specs=[
            pl.BlockSpec((tm, Q_LORA), lambda i: (i, 0)),
            pl.BlockSpec(w2.shape, lambda i: (0, 0)),
            pl.BlockSpec((tm, LANES), lambda i: (i % nt, 0)),
            pl.BlockSpec((tm, LANES), lambda i: (i % nt, 0)),
        ],
        out_specs=pl.BlockSpec((tm, N_HEADS * Q_HEAD_W), lambda i: (i, 0)),
        out_shape=jax.ShapeDtypeStruct((n, N_HEADS * Q_HEAD_W), BF16),
        compiler_params=_cparams(("parallel",)),
        name="proj_q",
    )(cqn, w2, cos_t, sin_t)


def _proj_kv_kernel(c_ref, w_ref, k_ref, v_ref):
    c = c_ref[...]
    hw = N_HEADS * QK_NOPE
    k_ref[...] = _dot(c, w_ref[:, :hw]).astype(BF16)
    v_ref[...] = _dot(c, w_ref[:, hw:]).astype(BF16)


def _proj_kv(ckv_b, w3, tm):
    n = ckv_b.shape[0]
    hw = N_HEADS * QK_NOPE
    return pl.pallas_call(
        _proj_kv_kernel,
        grid=(n // tm,),
        in_specs=[
            pl.BlockSpec((tm, KV_LORA), lambda i: (i, 0)),
            pl.BlockSpec(w3.shape, lambda i: (0, 0)),
        ],
        out_specs=[pl.BlockSpec((tm, hw), lambda i: (i, 0)),
                   pl.BlockSpec((tm, hw), lambda i: (i, 0))],
        out_shape=[jax.ShapeDtypeStruct((n, hw), BF16),
                   jax.ShapeDtypeStruct((n, hw), BF16)],
        compiler_params=_cparams(("parallel",)),
        name="proj_kv",
    )(ckv_b, w3)


def _attn_schedule(t, s, p, tq, tk):
    qi_l, ki_l, fl_l = [], [], []
    for qi in range(t // tq):
        q_lo = p + qi * tq
        q_hi = p + (qi + 1) * tq - 1
        kmax = min(s, (q_hi // CHUNK + 1) * CHUNK)
        full_vis = (q_lo // CHUNK + 1) * CHUNK
        nk = -(-kmax // tk)
        for ki in range(nk):
            need_mask = (ki + 1) * tk > full_vis
            qi_l.append(qi)
            ki_l.append(ki)
            fl_l.append(int(ki == 0) | (int(ki == nk - 1) << 1) | (int(need_mask) << 2))
    return (np.asarray(qi_l, np.int32), np.asarray(ki_l, np.int32), np.asarray(fl_l, np.int32))


def _attn_kernel(qi_tab, ki_tab, fl_tab, q_ref, kn_ref, kr_ref, v_ref, o_ref,
                 m_scr, l_scr, acc_scr, *, hg, tq, tk, past):
    step = pl.program_id(2)
    flags = fl_tab[step]
    qi = qi_tab[step]
    ki = ki_tab[step]

    @pl.when((flags & 1) != 0)
    def _():
        m_scr[...] = jnp.full(m_scr.shape, NEG_BIG, F32)
        l_scr[...] = jnp.zeros(l_scr.shape, F32)
        acc_scr[...] = jnp.zeros(acc_scr.shape, F32)

    def body(masked):
        if masked:
            qpos = past + qi * tq + lax.broadcasted_iota(jnp.int32, (tq, tk), 0)
            kpos = ki * tk + lax.broadcasted_iota(jnp.int32, (tq, tk), 1)
            visible = (kpos // CHUNK) <= (qpos // CHUNK)
        for h in range(hg):
            q = q_ref[0, :, h * Q_HEAD_W:(h + 1) * Q_HEAD_W]
            par = h % 2
            k = jnp.concatenate([kn_ref[0, :, h * QK_NOPE:(h + 1) * QK_NOPE],
                                 kr_ref[0, :, par * LANES:(par + 1) * LANES]], axis=1)
            sc = _dot_nt(q, k)
            if masked:
                sc = jnp.where(visible, sc, NEG_BIG)
            m_prev = m_scr[h]
            l_prev = l_scr[h]
            m_cur = jnp.max(sc, axis=1, keepdims=True)
            m_next = jnp.maximum(m_prev, m_cur)
            pexp = jnp.exp(sc - pltpu.repeat(m_next, tk // LANES, axis=1)) if tk % LANES == 0 \
                else jnp.exp(sc - m_next[:, :1])
            alpha = jnp.exp(m_prev - m_next)
            l_scr[h] = alpha * l_prev + jnp.sum(pexp, axis=1, keepdims=True)
            m_scr[h] = m_next
            pv = _dot(pexp.astype(BF16), v_ref[0, :, h * V_HEAD:(h + 1) * V_HEAD])
            acc_scr[h] = acc_scr[h] * alpha + pv

    @pl.when((flags & 4) != 0)
    def _():
        body(True)

    @pl.when((flags & 4) == 0)
    def _():
        body(False)

    @pl.when((flags & 2) != 0)
    def _():
        for h in range(hg):
            o_ref[0, :, h * V_HEAD:(h + 1) * V_HEAD] = (acc_scr[h] / l_scr[h]).astype(o_ref.dtype)


def _attention(q, kn, krpad, v, past, tq, tk, hg):
    b, t, _ = q.shape
    s = kn.shape[1]
    qi_np, ki_np, fl_np = _attn_schedule(t, s, past, tq, tk)
    n_steps = int(qi_np.shape[0])
    kernel = functools.partial(_attn_kernel, hg=hg, tq=tq, tk=tk, past=past)
    grid_spec = pltpu.PrefetchScalarGridSpec(
        num_scalar_prefetch=3,
        grid=(b, N_HEADS // hg, n_steps),
        in_specs=[
            pl.BlockSpec((1, tq, hg * Q_HEAD_W), lambda bi, g, st, qt, kt, ft: (bi, qt[st], g)),
            pl.BlockSpec((1, tk, hg * QK_NOPE), lambda bi, g, st, qt, kt, ft: (bi, kt[st], g)),
            pl.BlockSpec((1, tk, Q_HEAD_W), lambda bi, g, st, qt, kt, ft: (bi, kt[st], 0)),
            pl.BlockSpec((1, tk, hg * V_HEAD), lambda bi, g, st, qt, kt, ft: (bi, kt[st], g)),
        ],
        out_specs=pl.BlockSpec((1, tq, hg * V_HEAD), lambda bi, g, st, qt, kt, ft: (bi, qt[st], g)),
        scratch_shapes=[
            pltpu.VMEM((hg, tq, LANES), F32),
            pltpu.VMEM((hg, tq, LANES), F32),
            pltpu.VMEM((hg, tq, V_HEAD), F32),
        ],
    )
    return pl.pallas_call(
        kernel,
        grid_spec=grid_spec,
        out_shape=jax.ShapeDtypeStruct((b, t, N_HEADS * V_HEAD), BF16),
        compiler_params=_cparams(("parallel", "parallel", "arbitrary")),
        name="mla_attention",
    )(jnp.asarray(qi_np), jnp.asarray(ki_np), jnp.asarray(fl_np), q, kn, krpad, v)


def _proj_glu_kernel(xn_ref, wa_ref, wb_ref, wq_ref, glu_ref, qm_ref):
    xn = xn_ref[...]
    a = _dot(xn, wa_ref[...])
    bgate = _dot(xn, wb_ref[...])
    glu_ref[...] = a * jax.nn.sigmoid(bgate)
    qm_ref[...] = _dot(xn, wq_ref[...]).astype(BF16)


def _proj_glu(xn, w5, tm, tn):
    n = xn.shape[0]
    nj = CONV_DIM // tn
    return pl.pallas_call(
        _proj_glu_kernel,
        grid=(n // tm, nj),
        in_specs=[
            pl.BlockSpec((tm, D_MODEL), lambda i, j: (i, 0)),
            pl.BlockSpec((D_MODEL, tn), lambda i, j: (0, j)),
            pl.BlockSpec((D_MODEL, tn), lambda i, j: (0, nj + j)),
            pl.BlockSpec((D_MODEL, tn), lambda i, j: (0, 2 * nj + j)),
        ],
        out_specs=[pl.BlockSpec((tm, tn), lambda i, j: (i, j)),
                   pl.BlockSpec((tm, tn), lambda i, j: (i, j))],
        out_shape=[jax.ShapeDtypeStruct((n, CONV_DIM), F32),
                   jax.ShapeDtypeStruct((n, MEM_DIM), BF16)],
        compiler_params=_cparams(("parallel", "arbitrary")),
        name="proj_glu_qmem",
    )(xn, w5, w5, w5)


CONV_HALO = 32


def _conv_kernel(cur_ref, prev_ref, hist_ref, w_ref, b_ref, g_ref, beta_ref, o_ref, buf, *, tt):
    i = pl.program_id(1)
    halo = jnp.where(i == 0, hist_ref[0], prev_ref[0])
    buf[0:CONV_HALO, :] = halo
    buf[CONV_HALO:, :] = cur_ref[0]
    off = CONV_HALO - (CONV_WIDTH - 1)
    acc = jnp.zeros((tt, CONV_DIM), F32) + b_ref[...]
    for k in range(CONV_WIDTH):
        acc = acc + w_ref[k:k + 1, :] * buf[off + k:off + k + tt, :]
    mu = jnp.mean(acc, axis=-1, keepdims=True)
    d = acc - mu
    var = jnp.mean(d * d, axis=-1, keepdims=True)
    y = d * lax.rsqrt(var + EPS) * g_ref[...] + beta_ref[...]
    o_ref[0] = (y * jax.nn.sigmoid(y)).astype(o_ref.dtype)


def _conv_module(glu, hist, w_dw, b_dw, ln_g, ln_b, tt):
    b, t, _ = glu.shape
    r = tt // CONV_HALO
    kernel = functools.partial(_conv_kernel, tt=tt)
    return pl.pallas_call(
        kernel,
        grid=(b, t // tt),
        in_specs=[
            pl.BlockSpec((1, tt, CONV_DIM), lambda bi, i: (bi, i, 0)),
            pl.BlockSpec((1, CONV_HALO, CONV_DIM), lambda bi, i: (bi, jnp.maximum(i * r - 1, 0), 0)),
            pl.BlockSpec((1, CONV_HALO, CONV_DIM), lambda bi, i: (bi, 0, 0)),
            pl.BlockSpec(w_dw.shape, lambda bi, i: (0, 0)),
            pl.BlockSpec((1, CONV_DIM), lambda bi, i: (0, 0)),
            pl.BlockSpec((1, CONV_DIM), lambda bi, i: (0, 0)),
            pl.BlockSpec((1, CONV_DIM), lambda bi, i: (0, 0)),
        ],
        out_specs=pl.BlockSpec((1, tt, CONV_DIM), lambda bi, i: (bi, i, 0)),
        out_shape=jax.ShapeDtypeStruct((b, t, CONV_DIM), BF16),
        scratch_shapes=[pltpu.VMEM((tt + CONV_HALO, CONV_DIM), F32)],
        compiler_params=_cparams(("parallel", "parallel")),
        name="conv_module",
    )(glu, glu, hist, w_dw, b_dw, ln_g, ln_b)


def _mem_attn_kernel(q_ref, k_ref, v_ref, o_ref):
    for h in range(MEM_HEADS):
        sl = slice(h * MEM_HEAD_DIM, (h + 1) * MEM_HEAD_DIM)
        sc = _dot_nt(q_ref[0, :, sl], k_ref[0, :, sl]) * MEM_SCALE
        m = jnp.max(sc, axis=-1, keepdims=True)
        e = jnp.exp(sc - m)
        l = jnp.sum(e, axis=-1, keepdims=True)
        o = _dot(e.astype(BF16), v_ref[0, :, sl])
        o_ref[0, :, sl] = (o / l).astype(o_ref.dtype)


def _mem_attention(qm, mk, mv, tq):
    b, t, _ = qm.shape
    return pl.pallas_call(
        _mem_attn_kernel,
        grid=(b, t // tq),
        in_specs=[
            pl.BlockSpec((1, tq, MEM_DIM), lambda bi, i: (bi, i, 0)),
            pl.BlockSpec((1, MEM_LEN, MEM_DIM), lambda bi, i: (bi, 0, 0)),
            pl.BlockSpec((1, MEM_LEN, MEM_DIM), lambda bi, i: (bi, 0, 0)),
        ],
        out_specs=pl.BlockSpec((1, tq, MEM_DIM), lambda bi, i: (bi, i, 0)),
        out_shape=jax.ShapeDtypeStruct((b, t, MEM_DIM), BF16),
        compiler_params=_cparams(("parallel", "parallel")),
        name="mem_attention",
    )(qm, mk, mv)


def _norm_matmul_kernel(x_ref, g_ref, w_ref, o_ref):
    xn = _rms(x_ref[...], g_ref[...]).astype(BF16)
    o_ref[...] = _dot(xn, w_ref[...])


def _norm_matmul(x, g, w, tm, tn):
    n, kdim = x.shape
    nout = w.shape[1]
    return pl.pallas_call(
        _norm_matmul_kernel,
        grid=(n // tm, nout // tn),
        in_specs=[
            pl.BlockSpec((tm, kdim), lambda i, j: (i, 0)),
            pl.BlockSpec((1, kdim), lambda i, j: (0, 0)),
            pl.BlockSpec((kdim, tn), lambda i, j: (0, j)),
        ],
        out_specs=pl.BlockSpec((tm, tn), lambda i, j: (i, j)),
        out_shape=jax.ShapeDtypeStruct((n, nout), F32),
        compiler_params=_cparams(("parallel", "arbitrary")),
        name="norm_matmul",
    )(x, g, w)


def _merge_kernel(attn_ref, cv_ref, om_ref, xn_ref, x_ref, wa_ref, wb_ref, wc_ref,
                  wg0_ref, wg1_ref, wg2_ref, bg0_ref, bg1_ref, bg2_ref, wo_ref, gffn_ref,
                  h_ref, hn_ref, acc):
    j = pl.program_id(1)
    xn = xn_ref[...]
    a = _dot(attn_ref[...], wa_ref[...])
    bo = _dot(cv_ref[...], wb_ref[...])
    c = _dot(om_ref[...], wc_ref[...])
    g0 = jax.nn.sigmoid(_dot(xn, wg0_ref[...]) + bg0_ref[...])
    g1 = jax.nn.sigmoid(_dot(xn, wg1_ref[...]) + bg1_ref[...])
    g2 = jax.nn.sigmoid(_dot(xn, wg2_ref[...]) + bg2_ref[...])
    mix = (g0 * a + g1 * bo + g2 * c).astype(BF16)
    contrib = _dot(mix, wo_ref[...])

    @pl.when(j == 0)
    def _():
        acc[...] = x_ref[...] + contrib

    @pl.when(j != 0)
    def _():
        acc[...] += contrib

    @pl.when(j == pl.num_programs(1) - 1)
    def _():
        h = acc[...]
        h_ref[...] = h
        hn_ref[...] = _rms(h, gffn_ref[...]).astype(BF16)


def _merge(attn, cv, om, xn, x, w_mla_o, w_conv_o, w_mem_o, wg, b_gate, w_out, g_ffn, tm, tn):
    n = x.shape[0]
    nj = D_MODEL // tn
    row = lambda i, j: (i, 0)
    col = lambda i, j: (0, j)
    return pl.pallas_call(
        _merge_kernel,
        grid=(n // tm, nj),
        in_specs=[
            pl.BlockSpec((tm, N_HEADS * V_HEAD), row),
            pl.BlockSpec((tm, CONV_DIM), row),
            pl.BlockSpec((tm, MEM_DIM), row),
            pl.BlockSpec((tm, D_MODEL), row),
            pl.BlockSpec((tm, D_MODEL), row),
            pl.BlockSpec((N_HEADS * V_HEAD, tn), col),
            pl.BlockSpec((CONV_DIM, tn), col),
            pl.BlockSpec((MEM_DIM, tn), col),
            pl.BlockSpec((D_MODEL, tn), lambda i, j: (0, j)),
            pl.BlockSpec((D_MODEL, tn), lambda i, j: (0, nj + j)),
            pl.BlockSpec((D_MODEL, tn), lambda i, j: (0, 2 * nj + j)),
            pl.BlockSpec((1, tn), lambda i, j: (0, j)),
            pl.BlockSpec((1, tn), lambda i, j: (0, nj + j)),
            pl.BlockSpec((1, tn), lambda i, j: (0, 2 * nj + j)),
            pl.BlockSpec((tn, D_MODEL), lambda i, j: (j, 0)),
            pl.BlockSpec((1, D_MODEL), lambda i, j: (0, 0)),
        ],
        out_specs=[pl.BlockSpec((tm, D_MODEL), row), pl.BlockSpec((tm, D_MODEL), row)],
        out_shape=[jax.ShapeDtypeStruct((n, D_MODEL), F32),
                   jax.ShapeDtypeStruct((n, D_MODEL), BF16)],
        scratch_shapes=[pltpu.VMEM((tm, D_MODEL), F32)],
        compiler_params=_cparams(("parallel", "arbitrary")),
        name="merge_out",
    )(attn, cv, om, xn, x, w_mla_o, w_conv_o, w_mem_o, wg, wg, wg,
      b_gate, b_gate, b_gate, w_out, g_ffn)


FFN_HALO = 16


def _ffn_kernel(hn_ref, halo_ref, hist_ref, wa_ref, wv_ref, wdw_ref, bdw_ref, wd_ref, h_ref,
                gfin_ref, y_ref, acc, abuf, *, tm, t_len):
    i = pl.program_id(0)
    j = pl.program_id(1)
    hn = hn_ref[...]
    a = _dot(hn, wa_ref[...])
    val = _dot(hn, wv_ref[...])
    a_prev = _dot(halo_ref[...], wa_ref[...])
    at_batch_start = (i * tm) % t_len == 0
    abuf[0:FFN_HALO, :] = jnp.where(at_batch_start, hist_ref[0], a_prev)
    abuf[FFN_HALO:, :] = a
    conv = (wdw_ref[2:3, :] * a
            + wdw_ref[1:2, :] * abuf[FFN_HALO - 1:FFN_HALO - 1 + tm, :]
            + wdw_ref[0:1, :] * abuf[FFN_HALO - 2:FFN_HALO - 2 + tm, :]
            + bdw_ref[...])
    act = (conv * jax.nn.sigmoid(conv) * val).astype(BF16)
    contrib = _dot(act, wd_ref[...])

    @pl.when(j == 0)
    def _():
        acc[...] = h_ref[...] + contrib

    @pl.when(j != 0)
    def _():
        acc[...] += contrib

    @pl.when(j == pl.num_programs(1) - 1)
    def _():
        y_ref[...] = _rms(acc[...], gfin_ref[...])


def _ffn(hn, hist, w_up, w_dw, b_dw, w_down, h, g_final, t_len, tm, tn):
    n = hn.shape[0]
    nj = D_FF // tn
    r = tm // FFN_HALO
    kernel = functools.partial(_ffn_kernel, tm=tm, t_len=t_len)
    return pl.pallas_call(
        kernel,
        grid=(n // tm, nj),
        in_specs=[
            pl.BlockSpec((tm, D_MODEL), lambda i, j: (i, 0)),
            pl.BlockSpec((FFN_HALO, D_MODEL), lambda i, j: (jnp.maximum(i * r - 1, 0), 0)),
            pl.BlockSpec((1, FFN_HALO, tn), lambda i, j: ((i * tm) // t_len, 0, j)),
            pl.BlockSpec((D_MODEL, tn), lambda i, j: (0, j)),
            pl.BlockSpec((D_MODEL, tn), lambda i, j: (0, nj + j)),
            pl.BlockSpec((FFN_CONV_WIDTH, tn), lambda i, j: (0, j)),
            pl.BlockSpec((1, tn), lambda i, j: (0, j)),
            pl.BlockSpec((tn, D_MODEL), lambda i, j: (j, 0)),
            pl.BlockSpec((tm, D_MODEL), lambda i, j: (i, 0)),
            pl.BlockSpec((1, D_MODEL), lambda i, j: (0, 0)),
        ],
        out_specs=pl.BlockSpec((tm, D_MODEL), lambda i, j: (i, 0)),
        out_shape=jax.ShapeDtypeStruct((n, D_MODEL), F32),
        scratch_shapes=[pltpu.VMEM((tm, D_MODEL), F32), pltpu.VMEM((tm + FFN_HALO, tn), F32)],
        compiler_params=_cparams(("parallel", "arbitrary")),
        name="conv_ffn",
    )(hn, hn, hist, w_up, w_up, w_dw, b_dw, w_down, h, g_final)


def _rope_tables(pos):
    half = QK_ROPE // 2
    inv_freq = ROPE_THETA ** (-jnp.arange(half, dtype=F32) / half)
    ang = pos.astype(F32)[:, None] * inv_freq[None, :]
    cos = jnp.cos(ang)
    sin = jnp.sin(ang)
    cos_t = jnp.tile(cos, (1, LANES // half))
    sin_t = jnp.tile(jnp.concatenate([-sin, sin], axis=1), (1, LANES // QK_ROPE))
    return cos_t, sin_t


def _swap_halves(w):
    half = QK_ROPE // 2
    return jnp.concatenate([w[..., half:], w[..., :half]], axis=-1)


def _prep_weights(w_in, w_uq, w_ukv, w_mla_o, w_conv_o, w_mem_o, w_out, w_up, w_down,
                  w_mem_k, w_mem_v, w_conv_dw):
    kr = w_in[:, OFF_KR:OFF_GLU]
    kr_sw = _swap_halves(kr)
    w1 = jnp.concatenate([w_in[:, :OFF_KR], kr, kr, kr_sw, kr_sw], axis=1).astype(BF16)
    w5 = w_in[:, OFF_GLU:OFF_GATE].astype(BF16)
    wg = w_in[:, OFF_GATE:].astype(BF16)
    nope = w_uq[:, :, :QK_NOPE].reshape(Q_LORA, N_HEADS // 2, 2 * QK_NOPE)
    rope = w_uq[:, :, QK_NOPE:]
    rope_p = rope.reshape(Q_LORA, N_HEADS // 2, 2 * QK_ROPE)
    rope_sw = _swap_halves(rope).reshape(Q_LORA, N_HEADS // 2, 2 * QK_ROPE)
    w2 = jnp.concatenate([nope, rope_p, rope_sw], axis=2).reshape(Q_LORA, -1).astype(BF16)
    w3 = jnp.concatenate([w_ukv[:, :, :QK_NOPE].reshape(KV_LORA, -1),
                          w_ukv[:, :, QK_NOPE:].reshape(KV_LORA, -1)], axis=1).astype(BF16)
    wmem = jnp.concatenate([w_mem_k, w_mem_v], axis=1).astype(BF16)
    wdw = jnp.concatenate([w_conv_dw, jnp.zeros((1, CONV_DIM), F32)], axis=0)
    return dict(w1=w1, w5=w5, wg=wg, w2=w2, w3=w3, wmem=wmem, wdw=wdw,
                w_mla_o=w_mla_o.astype(BF16), w_conv_o=w_conv_o.astype(BF16),
                w_mem_o=w_mem_o.astype(BF16), w_out=w_out.astype(BF16),
                w_up=w_up.astype(BF16), w_down=w_down.astype(BF16))


def _pad_hist(hist, rows):
    b, k, c = hist.shape
    return jnp.concatenate([jnp.zeros((b, rows - k, c), hist.dtype), hist], axis=1)


def _encoder_layer(x, past_len, hist_conv, hist_ffn, past_ckv, past_krope, mem_k, mem_v, wts, prm,
                   cfg):
    b, t, _ = x.shape
    n = b * t
    pos = past_len + jnp.arange(t)
    cos_t, sin_t = _rope_tables(pos)
    tm = cfg["tm"]
    if tm > t:
        cos_t = jnp.tile(cos_t, (tm // t, 1))
        sin_t = jnp.tile(sin_t, (tm // t, 1))
    x2 = x.reshape(n, D_MODEL)

    xn, cqn, ckv, ckv_b, krope, krpad = _proj_lat(
        x2, prm["g_mix"], wts["w1"], prm["g_cq"], prm["g_ckv"], cos_t, sin_t, tm)
    q = _proj_q(cqn, wts["w2"], cos_t, sin_t, tm)

    if past_len:
        lane = jnp.arange(LANES)[None, None, :]
        pk = past_krope.astype(BF16)
        zeros = jnp.zeros_like(pk)
        past_pad = jnp.concatenate([pk, zeros, zeros, pk], axis=-1)
        del lane
        ckv_all = jnp.concatenate([past_ckv.astype(BF16), ckv_b.reshape(b, t, KV_LORA)], axis=1)
        kr_all = jnp.concatenate([past_pad, krpad.reshape(b, t, Q_HEAD_W)], axis=1)
    else:
        ckv_all = ckv_b.reshape(b, t, KV_LORA)
        kr_all = krpad.reshape(b, t, Q_HEAD_W)
    s = past_len + t
    kn, v = _proj_kv(ckv_all.reshape(b * s, KV_LORA), wts["w3"], _row_tile(b * s, cfg["tm_kv"]))
    attn = _attention(q.reshape(b, t, -1), kn.reshape(b, s, -1), kr_all, v.reshape(b, s, -1),
                      past_len, cfg["tq"], cfg["tk"], cfg["hg"])

    glu, qm = _proj_glu(xn, wts["w5"], tm, cfg["tn_glu"])
    glu3 = glu.reshape(b, t, CONV_DIM)
    cv = _conv_module(glu3, _pad_hist(hist_conv, CONV_HALO), wts["wdw"], prm["b_conv_dw"],
                      prm["ln_conv_g"], prm["ln_conv_b"], cfg["tt"])
    om = _mem_attention(qm.reshape(b, t, MEM_DIM), mem_k, mem_v, cfg["tq_mem"])

    h, hn = _merge(attn.reshape(n, -1), cv.reshape(n, CONV_DIM), om.reshape(n, MEM_DIM), xn, x2,
                   wts["w_mla_o"], wts["w_conv_o"], wts["w_mem_o"], wts["wg"], prm["b_gate"],
                   wts["w_out"], prm["g_ffn"], tm, cfg["tn_merge"])
    y = _ffn(hn, _pad_hist(hist_ffn, FFN_HALO), wts["w_up"], prm["w_ffn_dw"], prm["b_ffn_dw"],
             wts["w_down"], h, prm["g_final"], t, cfg["tm_ffn"], cfg["tn_ffn"])

    full_conv = jnp.concatenate([hist_conv, glu3], axis=1)
    new_hist_conv = full_conv[:, full_conv.shape[1] - (CONV_WIDTH - 1):]
    h_tail = h.reshape(b, t, D_MODEL)[:, t - 8:]
    return (y.reshape(b, t, D_MODEL), ckv.reshape(b, t, KV_LORA), krope.reshape(b, t, QK_ROPE),
            new_hist_conv, h_tail)


def kernel(x_prompt, x_sample, mem_prompt, cache_ckv, cache_krope, state_conv, state_ffn_conv,
           cache_mem_k, cache_mem_v, g_mix, w_in, g_cq, w_uq, g_ckv, w_ukv, w_mla_o,
           w_conv_dw, b_conv_dw, ln_conv_g, ln_conv_b, w_conv_o, g_mem, w_mem_k, w_mem_v,
           w_mem_o, b_gate, w_out, g_ffn, w_up, w_ffn_dw, b_ffn_dw, w_down, g_final):
    depth = g_mix.shape[0]
    assert depth == 1
    l = 0
    bp, tp, _ = x_prompt.shape
    bs, ts, _ = x_sample.shape
    past = cache_ckv.shape[2]
    assert tp >= CONV_WIDTH - 1 and ts >= CONV_WIDTH - 1 and ts >= 8

    wts = _prep_weights(w_in[l], w_uq[l], w_ukv[l], w_mla_o[l], w_conv_o[l], w_mem_o[l], w_out[l],
                        w_up[l], w_down[l], w_mem_k[l], w_mem_v[l], w_conv_dw[l])
    row = lambda a: a.reshape(1, -1)
    prm = dict(g_mix=row(g_mix[l]), g_cq=row(g_cq[l]), g_ckv=row(g_ckv[l]),
               b_conv_dw=row(b_conv_dw[l]), ln_conv_g=row(ln_conv_g[l]), ln_conv_b=row(ln_conv_b[l]),
               b_gate=row(b_gate[l]), g_ffn=row(g_ffn[l]), w_ffn_dw=w_ffn_dw[l],
               b_ffn_dw=row(b_ffn_dw[l]), g_final=row(g_final))

    mem2 = mem_prompt.reshape(bp * MEM_LEN, D_MODEL)
    mkv = _norm_matmul(mem2, row(g_mem[l]), wts["wmem"], 256, 512)
    mk_p = mkv[:, :MEM_DIM].reshape(bp, MEM_LEN, MEM_DIM)
    mv_p = mkv[:, MEM_DIM:].reshape(bp, MEM_LEN, MEM_DIM)

    cfg_p = dict(tm=512, tm_kv=512, tq=512, tk=512, hg=8, tn_glu=256, tt=256, tq_mem=512,
                 tn_merge=256, tm_ffn=512, tn_ffn=512)
    n_s = bs * ts
    cfg_s = dict(tm=n_s, tm_kv=512, tq=ts, tk=past + ts, hg=8, tn_glu=256, tt=ts, tq_mem=ts,
                 tn_merge=256, tm_ffn=ts, tn_ffn=512)

    zeros_conv = jnp.zeros((bp, CONV_WIDTH - 1, CONV_DIM), F32)
    zeros_ffn = jnp.zeros((bp, FFN_CONV_WIDTH - 1, D_FF), F32)
    yp, ckv_p, kr_p, cs_p, htail_p = _encoder_layer(
        x_prompt, 0, zeros_conv, zeros_ffn, None, None,
        mk_p.astype(BF16), mv_p.astype(BF16), wts, prm, cfg_p)
    ys, ckv_s, kr_s, cs_s, htail_s = _encoder_layer(
        x_sample, past, state_conv[l], state_ffn_conv[l], cache_ckv[l], cache_krope[l],
        cache_mem_k[l].reshape(bs, MEM_LEN, MEM_DIM).astype(BF16),
        cache_mem_v[l].reshape(bs, MEM_LEN, MEM_DIM).astype(BF16), wts, prm, cfg_s)

    tails = jnp.concatenate([htail_p.reshape(bp * 8, D_MODEL), htail_s.reshape(bs * 8, D_MODEL)], axis=0)
    a_tail = _norm_matmul(tails, prm["g_ffn"], wts["w_up"][:, :D_FF], tails.shape[0], 512)
    fs_p = a_tail[:bp * 8].reshape(bp, 8, D_FF)[:, 8 - (FFN_CONV_WIDTH - 1):]
    fs_s = a_tail[bp * 8:].reshape(bs, 8, D_FF)[:, 8 - (FFN_CONV_WIDTH - 1):]

    st = lambda a: a[None]
    return (yp, ys, st(ckv_p), st(kr_p), st(cs_p), st(fs_p),
            st(mk_p.reshape(bp, MEM_LEN, MEM_HEADS, MEM_HEAD_DIM)),
            st(mv_p.reshape(bp, MEM_LEN, MEM_HEADS, MEM_HEAD_DIM)),
            st(ckv_s), st(kr_s), st(cs_s), st(fs_s))
```

```python
import functools

import numpy as np
import jax
import jax.numpy as jnp
from jax import lax
from jax.experimental import pallas as pl
from jax.experimental.pallas import tpu as pltpu

D_MODEL = 2048
CHUNK = 64
EPS = 1e-6
N_HEADS = 16
QK_NOPE = 128
QK_ROPE = 64
V_HEAD = 128
Q_LORA = 512
KV_LORA = 512
ROPE_THETA = 10000.0
MLA_SCALE = (QK_NOPE + QK_ROPE) ** -0.5
CONV_DIM = 1024
CONV_WIDTH = 31
MEM_LEN = 256
MEM_HEADS = 4
MEM_HEAD_DIM = 256
MEM_DIM = MEM_HEADS * MEM_HEAD_DIM
MEM_SCALE = MEM_HEAD_DIM ** -0.5
N_BRANCH = 3
D_FF = 5632
FFN_CONV_WIDTH = 3
OFF_CKV = Q_LORA
OFF_KR = OFF_CKV + KV_LORA
OFF_GLU = OFF_KR + QK_ROPE
OFF_QM = OFF_GLU + 2 * CONV_DIM
OFF_GATE = OFF_QM + MEM_DIM

LANES = 128
V7X_VMEM_LIMIT = 56 * 1024 * 1024
Q_HEAD_W = 2 * LANES
V_HEAD_W = 2 * LANES
NEG_BIG = -1e30
Q_SCALE = MLA_SCALE * float(np.log2(np.e))

BF16 = jnp.bfloat16
F32 = jnp.float32


def _cparams(sem, vmem=V7X_VMEM_LIMIT):
    return pltpu.CompilerParams(dimension_semantics=sem, vmem_limit_bytes=vmem)


def _row_tile(n, cap, mult=16):
    t = min(cap, n)
    t -= t % mult
    while t > mult and n % t:
        t -= mult
    assert t >= mult and n % t == 0, (n, cap)
    return t


def _rms(x, g):
    y = x * lax.rsqrt(jnp.mean(x * x, axis=-1, keepdims=True) + EPS)
    return y * g


def _dot(a, b):
    return jnp.dot(a, b, preferred_element_type=F32)


def _dot_nt(a, b):
    return lax.dot_general(a, b, (((1,), (1,)), ((), ())), preferred_element_type=F32)


def _proj_lat_kernel(x_ref, gmix_ref, w_ref, gcq_ref, gckv_ref, cos_ref, sin_ref,
                     xn_ref, cqn_ref, ckv_ref, ckvb_ref, kr_ref, krpad_ref):
    xn = _rms(x_ref[...], gmix_ref[...]).astype(BF16)
    xn_ref[...] = xn
    z = _dot(xn, w_ref[...])
    cqn_ref[...] = _rms(z[:, :Q_LORA], gcq_ref[...]).astype(BF16)
    ckv = _rms(z[:, OFF_CKV:OFF_KR], gckv_ref[...])
    ckv_ref[...] = ckv
    ckvb_ref[...] = ckv.astype(BF16)
    r = z[:, OFF_KR:OFF_KR + LANES] * cos_ref[...] + z[:, OFF_KR + LANES:] * sin_ref[...]
    kr_ref[...] = r[:, :QK_ROPE]
    lane = lax.broadcasted_iota(jnp.int32, r.shape, 1)
    zero = jnp.zeros_like(r)
    krpad_ref[:, :LANES] = jnp.where(lane < QK_ROPE, r, zero).astype(BF16)
    krpad_ref[:, LANES:] = jnp.where(lane >= QK_ROPE, r, zero).astype(BF16)


def _proj_lat(x, g_mix, w1, g_cq, g_ckv, cos_t, sin_t, tm):
    n = x.shape[0]
    nt = cos_t.shape[0] // tm
    row = lambda i: (i, 0)
    const = lambda i: (0, 0)
    tab = lambda i: (i % nt, 0)
    return pl.pallas_call(
        _proj_lat_kernel,
        grid=(n // tm,),
        in_specs=[
            pl.BlockSpec((tm, D_MODEL), row),
            pl.BlockSpec((1, D_MODEL), const),
            pl.BlockSpec(w1.shape, const),
            pl.BlockSpec((1, Q_LORA), const),
            pl.BlockSpec((1, KV_LORA), const),
            pl.BlockSpec((tm, LANES), tab),
            pl.BlockSpec((tm, LANES), tab),
        ],
        out_specs=[
            pl.BlockSpec((tm, D_MODEL), row),
            pl.BlockSpec((tm, Q_LORA), row),
            pl.BlockSpec((tm, KV_LORA), row),
            pl.BlockSpec((tm, KV_LORA), row),
            pl.BlockSpec((tm, QK_ROPE), row),
            pl.BlockSpec((tm, Q_HEAD_W), row),
        ],
        out_shape=[
            jax.ShapeDtypeStruct((n, D_MODEL), BF16),
            jax.ShapeDtypeStruct((n, Q_LORA), BF16),
            jax.ShapeDtypeStruct((n, KV_LORA), F32),
            jax.ShapeDtypeStruct((n, KV_LORA), BF16),
            jax.ShapeDtypeStruct((n, QK_ROPE), F32),
            jax.ShapeDtypeStruct((n, Q_HEAD_W), BF16),
        ],
        compiler_params=_cparams(("parallel",)),
        name="proj_lat",
    )(x, g_mix, w1, g_cq, g_ckv, cos_t, sin_t)


def _proj_q_kernel(cqn_ref, w_ref, cos_ref, sin_ref, q_ref):
    cqn = cqn_ref[...]
    cos = cos_ref[...]
    sin = sin_ref[...]
    pw = 4 * LANES
    for p in range(N_HEADS // 2):
        r = _dot(cqn, w_ref[:, p * pw:(p + 1) * pw])
        rp = ((r[:, 2 * LANES:3 * LANES] * cos + r[:, 3 * LANES:] * sin) * Q_SCALE).astype(BF16)
        q_ref[:, p * pw:p * pw + LANES] = (r[:, :LANES] * Q_SCALE).astype(BF16)
        q_ref[:, p * pw + LANES:p * pw + 2 * LANES] = rp
        q_ref[:, p * pw + 2 * LANES:p * pw + 3 * LANES] = (r[:, LANES:2 * LANES] * Q_SCALE).astype(BF16)
        q_ref[:, p * pw + 3 * LANES:(p + 1) * pw] = rp


def _proj_q(cqn, w2, cos_t, sin_t, tm):
    n = cqn.shape[0]
    nt = cos_t.shape[0] // tm
    return pl.pallas_call(
        _proj_q_kernel,
        grid=(n // tm,),
        in_specs=[
            pl.BlockSpec((tm, Q_LORA), lambda i: (i, 0)),
            pl.BlockSpec(w2.shape, lambda i: (0, 0)),
            pl.BlockSpec((tm, LANES), lambda i: (i % nt, 0)),
            pl.BlockSpec((tm, LANES), lambda i: (i % nt, 0)),
        ],
        out_specs=pl.BlockSpec((tm, N_HEADS * Q_HEAD_W), lambda i: (i, 0)),
        out_shape=jax.ShapeDtypeStruct((n, N_HEADS * Q_HEAD_W), BF16),
        compiler_params=_cparams(("parallel",)),
        name="proj_q",
    )(cqn, w2, cos_t, sin_t)


def _proj_kv_kernel(c_ref, kr_ref, w_ref, k_ref, v_ref):
    c = c_ref[...]
    hw = N_HEADS * QK_NOPE
    kn = _dot(c, w_ref[:, :hw]).astype(BF16)
    vv = _dot(c, w_ref[:, hw:]).astype(BF16)
    ones = jnp.ones((c.shape[0], LANES), BF16)
    for h in range(N_HEADS):
        par = h % 2
        k_ref[:, h * Q_HEAD_W:h * Q_HEAD_W + QK_NOPE] = kn[:, h * QK_NOPE:(h + 1) * QK_NOPE]
        k_ref[:, h * Q_HEAD_W + QK_NOPE:(h + 1) * Q_HEAD_W] = kr_ref[:, par * LANES:(par + 1) * LANES]
        v_ref[:, h * V_HEAD_W:h * V_HEAD_W + V_HEAD] = vv[:, h * V_HEAD:(h + 1) * V_HEAD]
        v_ref[:, h * V_HEAD_W + V_HEAD:(h + 1) * V_HEAD_W] = ones


def _proj_kv(ckv_b, krpad, w3, tm):
    n = ckv_b.shape[0]
    return pl.pallas_call(
        _proj_kv_kernel,
        grid=(n // tm,),
        in_specs=[
            pl.BlockSpec((tm, KV_LORA), lambda i: (i, 0)),
            pl.BlockSpec((tm, Q_HEAD_W), lambda i: (i, 0)),
            pl.BlockSpec(w3.shape, lambda i: (0, 0)),
        ],
        out_specs=[pl.BlockSpec((tm, N_HEADS * Q_HEAD_W), lambda i: (i, 0)),
                   pl.BlockSpec((tm, N_HEADS * V_HEAD_W), lambda i: (i, 0))],
        out_shape=[jax.ShapeDtypeStruct((n, N_HEADS * Q_HEAD_W), BF16),
                   jax.ShapeDtypeStruct((n, N_HEADS * V_HEAD_W), BF16)],
        compiler_params=_cparams(("parallel",)),
        name="proj_kv",
    )(ckv_b, krpad, w3)


def _attn_schedule(t, s, p, tq, tk):
    qi_l, ki_l, fl_l = [], [], []
    for qi in range(t // tq):
        q_lo = p + qi * tq
        q_hi = p + (qi + 1) * tq - 1
        kmax = min(s, (q_hi // CHUNK + 1) * CHUNK)
        full_vis = (q_lo // CHUNK + 1) * CHUNK
        nk = -(-kmax // tk)
        for ki in range(nk):
            need_mask = (ki + 1) * tk > full_vis
            qi_l.append(qi)
            ki_l.append(ki)
            fl_l.append(int(ki == 0) | (int(ki == nk - 1) << 1) | (int(need_mask) << 2))
    return (np.asarray(qi_l, np.int32), np.asarray(ki_l, np.int32), np.asarray(fl_l, np.int32))


def _attn_kernel(qi_tab, ki_tab, fl_tab, q_ref, k_ref, v_ref, o_ref, m_scr, acc_scr,
                 *, hg, tq, tk, past):
    step = pl.program_id(2)
    flags = fl_tab[step]
    qi = qi_tab[step]
    ki = ki_tab[step]

    @pl.when((flags & 1) != 0)
    def _():
        m_scr[...] = jnp.full(m_scr.shape, NEG_BIG, F32)
        acc_scr[...] = jnp.zeros(acc_scr.shape, F32)

    def body(masked):
        if masked:
            qpos = past + qi * tq + lax.broadcasted_iota(jnp.int32, (tq, tk), 0)
            kpos = ki * tk + lax.broadcasted_iota(jnp.int32, (tq, tk), 1)
            visible = (kpos // CHUNK) <= (qpos // CHUNK)
        for h in range(hg):
            q = q_ref[0, :, h * Q_HEAD_W:(h + 1) * Q_HEAD_W]
            k = k_ref[0, :, h * Q_HEAD_W:(h + 1) * Q_HEAD_W]
            sc = _dot_nt(q, k)
            if masked:
                sc = jnp.where(visible, sc, NEG_BIG)
            m_prev = m_scr[h]
            m_cur = jnp.max(sc, axis=1, keepdims=True)
            m_next = jnp.maximum(m_prev, m_cur)
            if tk % LANES == 0:
                p = jnp.exp2(sc - pltpu.repeat(m_next, tk // LANES, axis=1))
            else:
                p = jnp.exp2(sc - m_next[:, :1])
            alpha = jnp.exp2(m_prev - m_next)
            m_scr[h] = m_next
            pv = _dot(p.astype(BF16), v_ref[0, :, h * V_HEAD_W:(h + 1) * V_HEAD_W])
            acc_scr[h] = acc_scr[h] * pltpu.repeat(alpha, V_HEAD_W // LANES, axis=1) + pv

    @pl.when((flags & 4) != 0)
    def _():
        body(True)

    @pl.when((flags & 4) == 0)
    def _():
        body(False)

    @pl.when((flags & 2) != 0)
    def _():
        for h in range(hg):
            acc = acc_scr[h]
            o_ref[0, :, h * V_HEAD:(h + 1) * V_HEAD] = (acc[:, :V_HEAD] / acc[:, V_HEAD:]).astype(o_ref.dtype)


def _attention(q, k, v, past, tq, tk, hg):
    b, t, _ = q.shape
    s = k.shape[1]
    qi_np, ki_np, fl_np = _attn_schedule(t, s, past, tq, tk)
    n_steps = int(qi_np.shape[0])
    kernel = functools.partial(_attn_kernel, hg=hg, tq=tq, tk=tk, past=past)
    grid_spec = pltpu.PrefetchScalarGridSpec(
        num_scalar_prefetch=3,
        grid=(b, N_HEADS // hg, n_steps),
        in_specs=[
            pl.BlockSpec((1, tq, hg * Q_HEAD_W), lambda bi, g, st, qt, kt, ft: (bi, qt[st], g)),
            pl.BlockSpec((1, tk, hg * Q_HEAD_W), lambda bi, g, st, qt, kt, ft: (bi, kt[st], g)),
            pl.BlockSpec((1, tk, hg * V_HEAD_W), lambda bi, g, st, qt, kt, ft: (bi, kt[st], g)),
        ],
        out_specs=pl.BlockSpec((1, tq, hg * V_HEAD), lambda bi, g, st, qt, kt, ft: (bi, qt[st], g)),
        scratch_shapes=[
            pltpu.VMEM((hg, tq, LANES), F32),
            pltpu.VMEM((hg, tq, V_HEAD_W), F32),
        ],
    )
    return pl.pallas_call(
        kernel,
        grid_spec=grid_spec,
        out_shape=jax.ShapeDtypeStruct((b, t, N_HEADS * V_HEAD), BF16),
        compiler_params=_cparams(("parallel", "parallel", "arbitrary")),
        name="mla_attention",
    )(jnp.asarray(qi_np), jnp.asarray(ki_np), jnp.asarray(fl_np), q, k, v)


def _proj_glu_kernel(xn_ref, wa_ref, wb_ref, wq_ref, glu_ref, qm_ref):
    xn = xn_ref[...]
    a = _dot(xn, wa_ref[...])
    bgate = _dot(xn, wb_ref[...])
    glu_ref[...] = a * jax.nn.sigmoid(bgate)
    qm_ref[...] = _dot(xn, wq_ref[...]).astype(BF16)


def _proj_glu(xn, w5, tm, tn):
    n = xn.shape[0]
    nj = CONV_DIM // tn
    return pl.pallas_call(
        _proj_glu_kernel,
        grid=(nj, n // tm),
        in_specs=[
            pl.BlockSpec((tm, D_MODEL), lambda j, i: (i, 0)),
            pl.BlockSpec((D_MODEL, tn), lambda j, i: (0, j)),
            pl.BlockSpec((D_MODEL, tn), lambda j, i: (0, nj + j)),
            pl.BlockSpec((D_MODEL, tn), lambda j, i: (0, 2 * nj + j)),
        ],
        out_specs=[pl.BlockSpec((tm, tn), lambda j, i: (i, j)),
                   pl.BlockSpec((tm, tn), lambda j, i: (i, j))],
        out_shape=[jax.ShapeDtypeStruct((n, CONV_DIM), F32),
                   jax.ShapeDtypeStruct((n, MEM_DIM), BF16)],
        compiler_params=_cparams(("parallel", "parallel")),
        name="proj_glu_qmem",
    )(xn, w5, w5, w5)


CONV_HALO = 32


SUBLANES = 8
CONV_ROWS = 32


def _conv_kernel(cur_ref, prev_ref, hist_ref, w_ref, b_ref, g_ref, beta_ref, o_ref, buf, shifted, *, tt):
    i = pl.program_id(1)
    halo = jnp.where(i == 0, hist_ref[0], prev_ref[0])
    buf[0:CONV_HALO, :] = halo
    buf[CONV_HALO:, :] = cur_ref[0]
    span = tt + CONV_HALO - SUBLANES
    for p in range(1, SUBLANES):
        shifted[p - 1] = buf[p:p + span, :]
    off = CONV_HALO - (CONV_WIDTH - 1)
    bias = b_ref[...]
    gam = g_ref[...]
    beta = beta_ref[...]
    for r0 in range(0, tt, CONV_ROWS):
        acc = jnp.zeros((CONV_ROWS, CONV_DIM), F32) + bias
        for k in range(CONV_WIDTH):
            s = off + k + r0
            p = s % SUBLANES
            if p == 0:
                rows = buf[s:s + CONV_ROWS, :]
            else:
                rows = shifted[p - 1, s - p:s - p + CONV_ROWS, :]
            acc = acc + w_ref[k:k + 1, :] * rows
        mu = jnp.mean(acc, axis=-1, keepdims=True)
        d = acc - mu
        var = jnp.mean(d * d, axis=-1, keepdims=True)
        y = d * lax.rsqrt(var + EPS) * gam + beta
        o_ref[0, r0:r0 + CONV_ROWS, :] = (y * jax.nn.sigmoid(y)).astype(o_ref.dtype)


def _conv_module(glu, hist, w_dw, b_dw, ln_g, ln_b, tt):
    b, t, _ = glu.shape
    r = tt // CONV_HALO
    assert tt % CONV_ROWS == 0
    kernel = functools.partial(_conv_kernel, tt=tt)
    return pl.pallas_call(
        kernel,
        grid=(b, t // tt),
        in_specs=[
            pl.BlockSpec((1, tt, CONV_DIM), lambda bi, i: (bi, i, 0)),
            pl.BlockSpec((1, CONV_HALO, CONV_DIM), lambda bi, i: (bi, jnp.maximum(i * r - 1, 0), 0)),
            pl.BlockSpec((1, CONV_HALO, CONV_DIM), lambda bi, i: (bi, 0, 0)),
            pl.BlockSpec(w_dw.shape, lambda bi, i: (0, 0)),
            pl.BlockSpec((1, CONV_DIM), lambda bi, i: (0, 0)),
            pl.BlockSpec((1, CONV_DIM), lambda bi, i: (0, 0)),
            pl.BlockSpec((1, CONV_DIM), lambda bi, i: (0, 0)),
        ],
        out_specs=pl.BlockSpec((1, tt, CONV_DIM), lambda bi, i: (bi, i, 0)),
        out_shape=jax.ShapeDtypeStruct((b, t, CONV_DIM), BF16),
        scratch_shapes=[pltpu.VMEM((tt + CONV_HALO, CONV_DIM), F32),
                        pltpu.VMEM((SUBLANES - 1, tt + CONV_HALO - SUBLANES, CONV_DIM), F32)],
        compiler_params=_cparams(("parallel", "parallel")),
        name="conv_module",
    )(glu, glu, hist, w_dw, b_dw, ln_g, ln_b)


def _mem_attn_kernel(q_ref, k_ref, v_ref, o_ref):
    for h in range(MEM_HEADS):
        sl = slice(h * MEM_HEAD_DIM, (h + 1) * MEM_HEAD_DIM)
        sc = _dot_nt(q_ref[0, :, sl], k_ref[0, :, sl]) * MEM_SCALE
        m = jnp.max(sc, axis=-1, keepdims=True)
        e = jnp.exp(sc - m)
        l = jnp.sum(e, axis=-1, keepdims=True)
        o = _dot(e.astype(BF16), v_ref[0, :, sl])
        o_ref[0, :, sl] = (o / l).astype(o_ref.dtype)


def _mem_attention(qm, mk, mv, tq):
    b, t, _ = qm.shape
    return pl.pallas_call(
        _mem_attn_kernel,
        grid=(b, t // tq),
        in_specs=[
            pl.BlockSpec((1, tq, MEM_DIM), lambda bi, i: (bi, i, 0)),
            pl.BlockSpec((1, MEM_LEN, MEM_DIM), lambda bi, i: (bi, 0, 0)),
            pl.BlockSpec((1, MEM_LEN, MEM_DIM), lambda bi, i: (bi, 0, 0)),
        ],
        out_specs=pl.BlockSpec((1, tq, MEM_DIM), lambda bi, i: (bi, i, 0)),
        out_shape=jax.ShapeDtypeStruct((b, t, MEM_DIM), BF16),
        compiler_params=_cparams(("parallel", "parallel")),
        name="mem_attention",
    )(qm, mk, mv)


def _norm_matmul_kernel(x_ref, g_ref, w_ref, o_ref):
    xn = _rms(x_ref[...], g_ref[...]).astype(BF16)
    o_ref[...] = _dot(xn, w_ref[...])


def _norm_matmul(x, g, w, tm, tn):
    n, kdim = x.shape
    nout = w.shape[1]
    return pl.pallas_call(
        _norm_matmul_kernel,
        grid=(n // tm, nout // tn),
        in_specs=[
            pl.BlockSpec((tm, kdim), lambda i, j: (i, 0)),
            pl.BlockSpec((1, kdim), lambda i, j: (0, 0)),
            pl.BlockSpec((kdim, tn), lambda i, j: (0, j)),
        ],
        out_specs=pl.BlockSpec((tm, tn), lambda i, j: (i, j)),
        out_shape=jax.ShapeDtypeStruct((n, nout), F32),
        compiler_params=_cparams(("parallel", "arbitrary")),
        name="norm_matmul",
    )(x, g, w)


MERGE_CHAINS = 2


def _merge_kernel(attn_ref, cv_ref, om_ref, xn_ref, x_ref, wa_ref, wb_ref, wc_ref,
                  wg0_ref, wg1_ref, wg2_ref, bg0_ref, bg1_ref, bg2_ref, wo_ref, gffn_ref,
                  h_ref, hn_ref):
    j = pl.program_id(1)

    @pl.when(j == 0)
    def _():
        h_ref[...] = x_ref[...]

    tm = h_ref.shape[0]
    rows = tm // MERGE_CHAINS
    for ch in range(MERGE_CHAINS):
        sl = slice(ch * rows, (ch + 1) * rows)
        xn = xn_ref[sl, :]
        a = _dot(attn_ref[sl, :], wa_ref[...])
        bo = _dot(cv_ref[sl, :], wb_ref[...])
        c = _dot(om_ref[sl, :], wc_ref[...])
        g0 = jax.nn.sigmoid(_dot(xn, wg0_ref[...]) + bg0_ref[...])
        g1 = jax.nn.sigmoid(_dot(xn, wg1_ref[...]) + bg1_ref[...])
        g2 = jax.nn.sigmoid(_dot(xn, wg2_ref[...]) + bg2_ref[...])
        mix = (g0 * a + g1 * bo + g2 * c).astype(BF16)
        h_ref[sl, :] += _dot(mix, wo_ref[...])

    @pl.when(j == pl.num_programs(1) - 1)
    def _():
        hn_ref[...] = _rms(h_ref[...], gffn_ref[...]).astype(BF16)


def _merge(attn, cv, om, xn, x, w_mla_o, w_conv_o, w_mem_o, wg, b_gate, w_out, g_ffn, tm, tn):
    n = x.shape[0]
    nj = D_MODEL // tn
    row = lambda i, j: (i, 0)
    col = lambda i, j: (0, j)
    return pl.pallas_call(
        _merge_kernel,
        grid=(n // tm, nj),
        in_specs=[
            pl.BlockSpec((tm, N_HEADS * V_HEAD), row),
            pl.BlockSpec((tm, CONV_DIM), row),
            pl.BlockSpec((tm, MEM_DIM), row),
            pl.BlockSpec((tm, D_MODEL), row),
            pl.BlockSpec((tm, D_MODEL), row),
            pl.BlockSpec((N_HEADS * V_HEAD, tn), col),
            pl.BlockSpec((CONV_DIM, tn), col),
            pl.BlockSpec((MEM_DIM, tn), col),
            pl.BlockSpec((D_MODEL, tn), lambda i, j: (0, j)),
            pl.BlockSpec((D_MODEL, tn), lambda i, j: (0, nj + j)),
            pl.BlockSpec((D_MODEL, tn), lambda i, j: (0, 2 * nj + j)),
            pl.BlockSpec((1, tn), lambda i, j: (0, j)),
            pl.BlockSpec((1, tn), lambda i, j: (0, nj + j)),
            pl.BlockSpec((1, tn), lambda i, j: (0, 2 * nj + j)),
            pl.BlockSpec((tn, D_MODEL), lambda i, j: (j, 0)),
            pl.BlockSpec((1, D_MODEL), lambda i, j: (0, 0)),
        ],
        out_specs=[pl.BlockSpec((tm, D_MODEL), row), pl.BlockSpec((tm, D_MODEL), row)],
        out_shape=[jax.ShapeDtypeStruct((n, D_MODEL), F32),
                   jax.ShapeDtypeStruct((n, D_MODEL), BF16)],
        compiler_params=_cparams(("parallel", "arbitrary")),
        name="merge_out",
    )(attn, cv, om, xn, x, w_mla_o, w_conv_o, w_mem_o, wg, wg, wg,
      b_gate, b_gate, b_gate, w_out, g_ffn)


FFN_HALO = 16
FFN_CHAINS = 2


def _ffn_kernel(hn_ref, halo_ref, hist_ref, wa_ref, wv_ref, wdw_ref, bdw_ref, wd_ref, h_ref,
                gfin_ref, y_ref, abuf, *, tm, t_len):
    i = pl.program_id(0)
    j = pl.program_id(1)

    @pl.when(j == 0)
    def _():
        y_ref[...] = h_ref[...]

    a_prev = _dot(halo_ref[...], wa_ref[...])
    at_batch_start = (i * tm) % t_len == 0
    abuf[0:FFN_HALO, :] = jnp.where(at_batch_start, hist_ref[0], a_prev)
    chains = FFN_CHAINS if tm % (FFN_CHAINS * FFN_HALO) == 0 else 1
    rows = tm // chains
    for ch in range(chains):
        r0 = ch * rows
        hn = hn_ref[r0:r0 + rows, :]
        a = _dot(hn, wa_ref[...])
        val = _dot(hn, wv_ref[...])
        abuf[FFN_HALO + r0:FFN_HALO + r0 + rows, :] = a
        conv = (wdw_ref[2:3, :] * a
                + wdw_ref[1:2, :] * abuf[FFN_HALO - 1 + r0:FFN_HALO - 1 + r0 + rows, :]
                + wdw_ref[0:1, :] * abuf[FFN_HALO - 2 + r0:FFN_HALO - 2 + r0 + rows, :]
                + bdw_ref[...])
        act = (conv * jax.nn.sigmoid(conv) * val).astype(BF16)
        y_ref[r0:r0 + rows, :] += _dot(act, wd_ref[...])

    @pl.when(j == pl.num_programs(1) - 1)
    def _():
        y_ref[...] = _rms(y_ref[...], gfin_ref[...])


def _ffn(hn, hist, w_up, w_dw, b_dw, w_down, h, g_final, t_len, tm, tn):
    n = hn.shape[0]
    nj = D_FF // tn
    r = tm // FFN_HALO
    kernel = functools.partial(_ffn_kernel, tm=tm, t_len=t_len)
    return pl.pallas_call(
        kernel,
        grid=(n // tm, nj),
        in_specs=[
            pl.BlockSpec((tm, D_MODEL), lambda i, j: (i, 0)),
            pl.BlockSpec((FFN_HALO, D_MODEL), lambda i, j: (jnp.maximum(i * r - 1, 0), 0)),
            pl.BlockSpec((1, FFN_HALO, tn), lambda i, j: ((i * tm) // t_len, 0, j)),
            pl.BlockSpec((D_MODEL, tn), lambda i, j: (0, j)),
            pl.BlockSpec((D_MODEL, tn), lambda i, j: (0, nj + j)),
            pl.BlockSpec((FFN_CONV_WIDTH, tn), lambda i, j: (0, j)),
            pl.BlockSpec((1, tn), lambda i, j: (0, j)),
            pl.BlockSpec((tn, D_MODEL), lambda i, j: (j, 0)),
            pl.BlockSpec((tm, D_MODEL), lambda i, j: (i, 0)),
            pl.BlockSpec((1, D_MODEL), lambda i, j: (0, 0)),
        ],
        out_specs=pl.BlockSpec((tm, D_MODEL), lambda i, j: (i, 0)),
        out_shape=jax.ShapeDtypeStruct((n, D_MODEL), F32),
        scratch_shapes=[pltpu.VMEM((tm + FFN_HALO, tn), F32)],
        compiler_params=_cparams(("parallel", "arbitrary")),
        name="conv_ffn",
    )(hn, hn, hist, w_up, w_up, w_dw, b_dw, w_down, h, g_final)


def _rope_tables(pos):
    half = QK_ROPE // 2
    inv_freq = ROPE_THETA ** (-jnp.arange(half, dtype=F32) / half)
    ang = pos.astype(F32)[:, None] * inv_freq[None, :]
    cos = jnp.cos(ang)
    sin = jnp.sin(ang)
    cos_t = jnp.tile(cos, (1, LANES // half))
    sin_t = jnp.tile(jnp.concatenate([-sin, sin], axis=1), (1, LANES // QK_ROPE))
    return cos_t, sin_t


def _swap_halves(w):
    half = QK_ROPE // 2
    return jnp.concatenate([w[..., half:], w[..., :half]], axis=-1)


def _prep_weights(w_in, w_uq, w_ukv, w_mla_o, w_conv_o, w_mem_o, w_out, w_up, w_down,
                  w_mem_k, w_mem_v, w_conv_dw):
    kr = w_in[:, OFF_KR:OFF_GLU]
    kr_sw = _swap_halves(kr)
    w1 = jnp.concatenate([w_in[:, :OFF_KR], kr, kr, kr_sw, kr_sw], axis=1).astype(BF16)
    w5 = w_in[:, OFF_GLU:OFF_GATE].astype(BF16)
    wg = w_in[:, OFF_GATE:].astype(BF16)
    nope = w_uq[:, :, :QK_NOPE].reshape(Q_LORA, N_HEADS // 2, 2 * QK_NOPE)
    rope = w_uq[:, :, QK_NOPE:]
    rope_p = rope.reshape(Q_LORA, N_HEADS // 2, 2 * QK_ROPE)
    rope_sw = _swap_halves(rope).reshape(Q_LORA, N_HEADS // 2, 2 * QK_ROPE)
    w2 = jnp.concatenate([nope, rope_p, rope_sw], axis=2).reshape(Q_LORA, -1).astype(BF16)
    w3 = jnp.concatenate([w_ukv[:, :, :QK_NOPE].reshape(KV_LORA, -1),
                          w_ukv[:, :, QK_NOPE:].reshape(KV_LORA, -1)], axis=1).astype(BF16)
    wmem = jnp.concatenate([w_mem_k, w_mem_v], axis=1).astype(BF16)
    wdw = jnp.concatenate([w_conv_dw, jnp.zeros((1, CONV_DIM), F32)], axis=0)
    return dict(w1=w1, w5=w5, wg=wg, w2=w2, w3=w3, wmem=wmem, wdw=wdw,
                w_mla_o=w_mla_o.astype(BF16), w_conv_o=w_conv_o.astype(BF16),
                w_mem_o=w_mem_o.astype(BF16), w_out=w_out.astype(BF16),
                w_up=w_up.astype(BF16), w_down=w_down.astype(BF16))


def _pad_hist(hist, rows):
    b, k, c = hist.shape
    return jnp.concatenate([jnp.zeros((b, rows - k, c), hist.dtype), hist], axis=1)


def _encoder_layer(x, past_len, hist_conv, hist_ffn, past_ckv, past_krope, mem_k, mem_v, wts, prm,
                   cfg):
    b, t, _ = x.shape
    n = b * t
    pos = past_len + jnp.arange(t)
    cos_t, sin_t = _rope_tables(pos)
    tm = cfg["tm"]
    if tm > t:
        cos_t = jnp.tile(cos_t, (tm // t, 1))
        sin_t = jnp.tile(sin_t, (tm // t, 1))
    x2 = x.reshape(n, D_MODEL)

    xn, cqn, ckv, ckv_b, krope, krpad = _proj_lat(
        x2, prm["g_mix"], wts["w1"], prm["g_cq"], prm["g_ckv"], cos_t, sin_t, tm)
    q = _proj_q(cqn, wts["w2"], cos_t, sin_t, tm)

    if past_len:
        pk = past_krope.astype(BF16)
        zeros = jnp.zeros_like(pk)
        past_pad = jnp.concatenate([pk, zeros, zeros, pk], axis=-1)
        ckv_all = jnp.concatenate([past_ckv.astype(BF16), ckv_b.reshape(b, t, KV_LORA)], axis=1)
        kr_all = jnp.concatenate([past_pad, krpad.reshape(b, t, Q_HEAD_W)], axis=1)
    else:
        ckv_all = ckv_b.reshape(b, t, KV_LORA)
        kr_all = krpad.reshape(b, t, Q_HEAD_W)
    s = past_len + t
    kf, vf = _proj_kv(ckv_all.reshape(b * s, KV_LORA), kr_all.reshape(b * s, Q_HEAD_W), wts["w3"],
                      _row_tile(b * s, cfg["tm_kv"]))
    attn = _attention(q.reshape(b, t, -1), kf.reshape(b, s, -1), vf.reshape(b, s, -1),
                      past_len, cfg["tq"], cfg["tk"], cfg["hg"])

    glu, qm = _proj_glu(xn, wts["w5"], cfg["tm_glu"], cfg["tn_glu"])
    glu3 = glu.reshape(b, t, CONV_DIM)
    cv = _conv_module(glu3, _pad_hist(hist_conv, CONV_HALO), wts["wdw"], prm["b_conv_dw"],
                      prm["ln_conv_g"], prm["ln_conv_b"], cfg["tt"])
    om = _mem_attention(qm.reshape(b, t, MEM_DIM), mem_k, mem_v, cfg["tq_mem"])

    h, hn = _merge(attn.reshape(n, -1), cv.reshape(n, CONV_DIM), om.reshape(n, MEM_DIM), xn, x2,
                   wts["w_mla_o"], wts["w_conv_o"], wts["w_mem_o"], wts["wg"], prm["b_gate"],
                   wts["w_out"], prm["g_ffn"], tm, cfg["tn_merge"])
    y = _ffn(hn, _pad_hist(hist_ffn, FFN_HALO), wts["w_up"], prm["w_ffn_dw"], prm["b_ffn_dw"],
             wts["w_down"], h, prm["g_final"], t, cfg["tm_ffn"], cfg["tn_ffn"])

    full_conv = jnp.concatenate([hist_conv, glu3], axis=1)
    new_hist_conv = full_conv[:, full_conv.shape[1] - (CONV_WIDTH - 1):]
    h_tail = h.reshape(b, t, D_MODEL)[:, t - 8:]
    return (y.reshape(b, t, D_MODEL), ckv.reshape(b, t, KV_LORA), krope.reshape(b, t, QK_ROPE),
            new_hist_conv, h_tail)


def kernel(x_prompt, x_sample, mem_prompt, cache_ckv, cache_krope, state_conv, state_ffn_conv,
           cache_mem_k, cache_mem_v, g_mix, w_in, g_cq, w_uq, g_ckv, w_ukv, w_mla_o,
           w_conv_dw, b_conv_dw, ln_conv_g, ln_conv_b, w_conv_o, g_mem, w_mem_k, w_mem_v,
           w_mem_o, b_gate, w_out, g_ffn, w_up, w_ffn_dw, b_ffn_dw, w_down, g_final):
    depth = g_mix.shape[0]
    assert depth == 1
    l = 0
    bp, tp, _ = x_prompt.shape
    bs, ts, _ = x_sample.shape
    past = cache_ckv.shape[2]
    assert tp >= CONV_WIDTH - 1 and ts >= CONV_WIDTH - 1 and ts >= 8

    wts = _prep_weights(w_in[l], w_uq[l], w_ukv[l], w_mla_o[l], w_conv_o[l], w_mem_o[l], w_out[l],
                        w_up[l], w_down[l], w_mem_k[l], w_mem_v[l], w_conv_dw[l])
    row = lambda a: a.reshape(1, -1)
    prm = dict(g_mix=row(g_mix[l]), g_cq=row(g_cq[l]), g_ckv=row(g_ckv[l]),
               b_conv_dw=row(b_conv_dw[l]), ln_conv_g=row(ln_conv_g[l]), ln_conv_b=row(ln_conv_b[l]),
               b_gate=row(b_gate[l]), g_ffn=row(g_ffn[l]), w_ffn_dw=w_ffn_dw[l],
               b_ffn_dw=row(b_ffn_dw[l]), g_final=row(g_final))

    mem2 = mem_prompt.reshape(bp * MEM_LEN, D_MODEL)
    mkv = _norm_matmul(mem2, row(g_mem[l]), wts["wmem"], 256, 512)
    mk_p = mkv[:, :MEM_DIM].reshape(bp, MEM_LEN, MEM_DIM)
    mv_p = mkv[:, MEM_DIM:].reshape(bp, MEM_LEN, MEM_DIM)

    cfg_p = dict(tm=512, tm_kv=512, tq=512, tk=512, hg=8, tm_glu=1024, tn_glu=512, tt=256, tq_mem=512,
                 tn_merge=256, tm_ffn=512, tn_ffn=512)
    n_s = bs * ts
    cfg_s = dict(tm=n_s, tm_kv=512, tq=ts, tk=past + ts, hg=8, tm_glu=n_s, tn_glu=512, tt=ts, tq_mem=ts,
                 tn_merge=256, tm_ffn=ts, tn_ffn=512)

    zeros_conv = jnp.zeros((bp, CONV_WIDTH - 1, CONV_DIM), F32)
    zeros_ffn = jnp.zeros((bp, FFN_CONV_WIDTH - 1, D_FF), F32)
    yp, ckv_p, kr_p, cs_p, htail_p = _encoder_layer(
        x_prompt, 0, zeros_conv, zeros_ffn, None, None,
        mk_p.astype(BF16), mv_p.astype(BF16), wts, prm, cfg_p)
    ys, ckv_s, kr_s, cs_s, htail_s = _encoder_layer(
        x_sample, past, state_conv[l], state_ffn_conv[l], cache_ckv[l], cache_krope[l],
        cache_mem_k[l].reshape(bs, MEM_LEN, MEM_DIM).astype(BF16),
        cache_mem_v[l].reshape(bs, MEM_LEN, MEM_DIM).astype(BF16), wts, prm, cfg_s)

    tails = jnp.concatenate([htail_p.reshape(bp * 8, D_MODEL), htail_s.reshape(bs * 8, D_MODEL)], axis=0)
    a_tail = _norm_matmul(tails, prm["g_ffn"], wts["w_up"][:, :D_FF], tails.shape[0], 512)
    fs_p = a_tail[:bp * 8].reshape(bp, 8, D_FF)[:, 8 - (FFN_CONV_WIDTH - 1):]
    fs_s = a_tail[bp * 8:].reshape(bs, 8, D_FF)[:, 8 - (FFN_CONV_WIDTH - 1):]

    st = lambda a: a[None]
    return (yp, ys, st(ckv_p), st(kr_p), st(cs_p), st(fs_p),
            st(mk_p.reshape(bp, MEM_LEN, MEM_HEADS, MEM_HEAD_DIM)),
            st(mv_p.reshape(bp, MEM_LEN, MEM_HEADS, MEM_HEAD_DIM)),
            st(ckv_s), st(kr_s), st(cs_s), st(fs_s))
```

```python
import functools

import numpy as np
import jax
import jax.numpy as jnp
from jax import lax
from jax.experimental import pallas as pl
from jax.experimental.pallas import tpu as pltpu

D_MODEL = 2048
CHUNK = 64
EPS = 1e-6
N_HEADS = 16
QK_NOPE = 128
QK_ROPE = 64
V_HEAD = 128
Q_LORA = 512
KV_LORA = 512
ROPE_THETA = 10000.0
MLA_SCALE = (QK_NOPE + QK_ROPE) ** -0.5
CONV_DIM = 1024
CONV_WIDTH = 31
MEM_LEN = 256
MEM_HEADS = 4
MEM_HEAD_DIM = 256
MEM_DIM = MEM_HEADS * MEM_HEAD_DIM
MEM_SCALE = MEM_HEAD_DIM ** -0.5
N_BRANCH = 3
D_FF = 5632
FFN_CONV_WIDTH = 3
OFF_CKV = Q_LORA
OFF_KR = OFF_CKV + KV_LORA
OFF_GLU = OFF_KR + QK_ROPE
OFF_QM = OFF_GLU + 2 * CONV_DIM
OFF_GATE = OFF_QM + MEM_DIM

LANES = 128
V7X_VMEM_LIMIT = 56 * 1024 * 1024
Q_HEAD_W = 2 * LANES
V_HEAD_W = 2 * LANES
NEG_BIG = -1e30
Q_SCALE = MLA_SCALE * float(np.log2(np.e))

BF16 = jnp.bfloat16
F32 = jnp.float32


def _cparams(sem, vmem=V7X_VMEM_LIMIT):
    return pltpu.CompilerParams(dimension_semantics=sem, vmem_limit_bytes=vmem)


def _row_tile(n, cap, mult=16):
    t = min(cap, n)
    t -= t % mult
    while t > mult and n % t:
        t -= mult
    assert t >= mult and n % t == 0, (n, cap)
    return t


def _rms(x, g):
    y = x * lax.rsqrt(jnp.mean(x * x, axis=-1, keepdims=True) + EPS)
    return y * g


def _dot(a, b):
    return jnp.dot(a, b, preferred_element_type=F32)


def _dot_nt(a, b):
    return lax.dot_general(a, b, (((1,), (1,)), ((), ())), preferred_element_type=F32)


def _proj_lat_kernel(x_ref, gmix_ref, w_ref, gcq_ref, gckv_ref, cos_ref, sin_ref,
                     xn_ref, cqn_ref, ckv_ref, ckvb_ref, kr_ref, krpad_ref):
    xn = _rms(x_ref[...], gmix_ref[...]).astype(BF16)
    xn_ref[...] = xn
    z = _dot(xn, w_ref[...])
    cqn_ref[...] = _rms(z[:, :Q_LORA], gcq_ref[...]).astype(BF16)
    ckv = _rms(z[:, OFF_CKV:OFF_KR], gckv_ref[...])
    ckv_ref[...] = ckv
    ckvb_ref[...] = ckv.astype(BF16)
    r = z[:, OFF_KR:OFF_KR + LANES] * cos_ref[...] + z[:, OFF_KR + LANES:] * sin_ref[...]
    kr_ref[...] = r[:, :QK_ROPE]
    lane = lax.broadcasted_iota(jnp.int32, r.shape, 1)
    zero = jnp.zeros_like(r)
    krpad_ref[:, :LANES] = jnp.where(lane < QK_ROPE, r, zero).astype(BF16)
    krpad_ref[:, LANES:] = jnp.where(lane >= QK_ROPE, r, zero).astype(BF16)


def _proj_lat(x, g_mix, w1, g_cq, g_ckv, cos_t, sin_t, tm):
    n = x.shape[0]
    nt = cos_t.shape[0] // tm
    row = lambda i: (i, 0)
    const = lambda i: (0, 0)
    tab = lambda i: (i % nt, 0)
    return pl.pallas_call(
        _proj_lat_kernel,
        grid=(n // tm,),
        in_specs=[
            pl.BlockSpec((tm, D_MODEL), row),
            pl.BlockSpec((1, D_MODEL), const),
            pl.BlockSpec(w1.shape, const),
            pl.BlockSpec((1, Q_LORA), const),
            pl.BlockSpec((1, KV_LORA), const),
            pl.BlockSpec((tm, LANES), tab),
            pl.BlockSpec((tm, LANES), tab),
        ],
        out_specs=[
            pl.BlockSpec((tm, D_MODEL), row),
            pl.BlockSpec((tm, Q_LORA), row),
            pl.BlockSpec((tm, KV_LORA), row),
            pl.BlockSpec((tm, KV_LORA), row),
            pl.BlockSpec((tm, QK_ROPE), row),
            pl.BlockSpec((tm, Q_HEAD_W), row),
        ],
        out_shape=[
            jax.ShapeDtypeStruct((n, D_MODEL), BF16),
            jax.ShapeDtypeStruct((n, Q_LORA), BF16),
            jax.ShapeDtypeStruct((n, KV_LORA), F32),
            jax.ShapeDtypeStruct((n, KV_LORA), BF16),
            jax.ShapeDtypeStruct((n, QK_ROPE), F32),
            jax.ShapeDtypeStruct((n, Q_HEAD_W), BF16),
        ],
        compiler_params=_cparams(("parallel",)),
        name="proj_lat",
    )(x, g_mix, w1, g_cq, g_ckv, cos_t, sin_t)


def _proj_q_kernel(cqn_ref, w_ref, cos_ref, sin_ref, q_ref):
    cqn = cqn_ref[...]
    cos = cos_ref[...]
    sin = sin_ref[...]
    pw = 4 * LANES
    for p in range(N_HEADS // 2):
        r = _dot(cqn, w_ref[:, p * pw:(p + 1) * pw])
        rp = ((r[:, 2 * LANES:3 * LANES] * cos + r[:, 3 * LANES:] * sin) * Q_SCALE).astype(BF16)
        q_ref[:, p * pw:p * pw + LANES] = (r[:, :LANES] * Q_SCALE).astype(BF16)
        q_ref[:, p * pw + LANES:p * pw + 2 * LANES] = rp
        q_ref[:, p * pw + 2 * LANES:p * pw + 3 * LANES] = (r[:, LANES:2 * LANES] * Q_SCALE).astype(BF16)
        q_ref[:, p * pw + 3 * LANES:(p + 1) * pw] = rp


def _proj_q(cqn, w2, cos_t, sin_t, tm):
    n = cqn.shape[0]
    nt = cos_t.shape[0] // tm
    return pl.pallas_call(
        _proj_q_kernel,
        grid=(n // tm,),
        in_specs=[
            pl.BlockSpec((tm, Q_LORA), lambda i: (i, 0)),
            pl.BlockSpec(w2.shape, lambda i: (0, 0)),
            pl.BlockSpec((tm, LANES), lambda i: (i % nt, 0)),
            pl.BlockSpec((tm, LANES), lambda i: (i % nt, 0)),
        ],
        out_specs=pl.BlockSpec((tm, N_HEADS * Q_HEAD_W), lambda i: (i, 0)),
        out_shape=jax.ShapeDtypeStruct((n, N_HEADS * Q_HEAD_W), BF16),
        compiler_params=_cparams(("parallel",)),
        name="proj_q",
    )(cqn, w2, cos_t, sin_t)


def _proj_kv_kernel(c_ref, kr_ref, w_ref, k_ref, v_ref):
    c = c_ref[...]
    hw = N_HEADS * QK_NOPE
    kn = _dot(c, w_ref[:, :hw]).astype(BF16)
    vv = _dot(c, w_ref[:, hw:]).astype(BF16)
    ones = jnp.ones((c.shape[0], LANES), BF16)
    for h in range(N_HEADS):
        par = h % 2
        k_ref[:, h * Q_HEAD_W:h * Q_HEAD_W + QK_NOPE] = kn[:, h * QK_NOPE:(h + 1) * QK_NOPE]
        k_ref[:, h * Q_HEAD_W + QK_NOPE:(h + 1) * Q_HEAD_W] = kr_ref[:, par * LANES:(par + 1) * LANES]
        v_ref[:, h * V_HEAD_W:h * V_HEAD_W + V_HEAD] = vv[:, h * V_HEAD:(h + 1) * V_HEAD]
        v_ref[:, h * V_HEAD_W + V_HEAD:(h + 1) * V_HEAD_W] = ones


def _proj_kv(ckv_b, krpad, w3, tm):
    n = ckv_b.shape[0]
    return pl.pallas_call(
        _proj_kv_kernel,
        grid=(n // tm,),
        in_specs=[
            pl.BlockSpec((tm, KV_LORA), lambda i: (i, 0)),
            pl.BlockSpec((tm, Q_HEAD_W), lambda i: (i, 0)),
            pl.BlockSpec(w3.shape, lambda i: (0, 0)),
        ],
        out_specs=[pl.BlockSpec((tm, N_HEADS * Q_HEAD_W), lambda i: (i, 0)),
                   pl.BlockSpec((tm, N_HEADS * V_HEAD_W), lambda i: (i, 0))],
        out_shape=[jax.ShapeDtypeStruct((n, N_HEADS * Q_HEAD_W), BF16),
                   jax.ShapeDtypeStruct((n, N_HEADS * V_HEAD_W), BF16)],
        compiler_params=_cparams(("parallel",)),
        name="proj_kv",
    )(ckv_b, krpad, w3)


def _attn_schedule(t, s, p, tq, tk):
    qi_l, ki_l, fl_l = [], [], []
    for qi in range(t // tq):
        q_lo = p + qi * tq
        q_hi = p + (qi + 1) * tq - 1
        kmax = min(s, (q_hi // CHUNK + 1) * CHUNK)
        full_vis = (q_lo // CHUNK + 1) * CHUNK
        nk = -(-kmax // tk)
        for ki in range(nk):
            need_mask = (ki + 1) * tk > full_vis
            qi_l.append(qi)
            ki_l.append(ki)
            fl_l.append(int(ki == 0) | (int(ki == nk - 1) << 1) | (int(need_mask) << 2))
    return (np.asarray(qi_l, np.int32), np.asarray(ki_l, np.int32), np.asarray(fl_l, np.int32))


def _attn_kernel(qi_tab, ki_tab, fl_tab, q_ref, k_ref, v_ref, o_ref, m_scr, acc_scr,
                 *, hg, tq, tk, past):
    step = pl.program_id(2)
    flags = fl_tab[step]
    qi = qi_tab[step]
    ki = ki_tab[step]

    @pl.when((flags & 1) != 0)
    def _():
        m_scr[...] = jnp.full(m_scr.shape, NEG_BIG, F32)
        acc_scr[...] = jnp.zeros(acc_scr.shape, F32)

    def body(masked):
        if masked:
            qpos = past + qi * tq + lax.broadcasted_iota(jnp.int32, (tq, tk), 0)
            kpos = ki * tk + lax.broadcasted_iota(jnp.int32, (tq, tk), 1)
            visible = (kpos // CHUNK) <= (qpos // CHUNK)
        for h in range(hg):
            q = q_ref[0, :, h * Q_HEAD_W:(h + 1) * Q_HEAD_W]
            k = k_ref[0, :, h * Q_HEAD_W:(h + 1) * Q_HEAD_W]
            sc = _dot_nt(q, k)
            if masked:
                sc = jnp.where(visible, sc, NEG_BIG)
            m_prev = m_scr[h]
            m_cur = jnp.max(sc, axis=1, keepdims=True)
            m_next = jnp.maximum(m_prev, m_cur)
            if tk % LANES == 0:
                p = jnp.exp2(sc - jnp.concatenate([m_next] * (tk // LANES), axis=1))
            else:
                p = jnp.exp2(sc - m_next[:, :1])
            alpha = jnp.exp2(m_prev - m_next)
            m_scr[h] = m_next
            pv = _dot(p.astype(BF16), v_ref[0, :, h * V_HEAD_W:(h + 1) * V_HEAD_W])
            acc_scr[h] = acc_scr[h] * jnp.concatenate([alpha] * (V_HEAD_W // LANES), axis=1) + pv

    @pl.when((flags & 4) != 0)
    def _():
        body(True)

    @pl.when((flags & 4) == 0)
    def _():
        body(False)

    @pl.when((flags & 2) != 0)
    def _():
        for h in range(hg):
            acc = acc_scr[h]
            o_ref[0, :, h * V_HEAD:(h + 1) * V_HEAD] = (acc[:, :V_HEAD] / acc[:, V_HEAD:]).astype(o_ref.dtype)


def _attention(q, k, v, past, tq, tk, hg):
    b, t, _ = q.shape
    s = k.shape[1]
    qi_np, ki_np, fl_np = _attn_schedule(t, s, past, tq, tk)
    n_steps = int(qi_np.shape[0])
    kernel = functools.partial(_attn_kernel, hg=hg, tq=tq, tk=tk, past=past)
    grid_spec = pltpu.PrefetchScalarGridSpec(
        num_scalar_prefetch=3,
        grid=(b, N_HEADS // hg, n_steps),
        in_specs=[
            pl.BlockSpec((1, tq, hg * Q_HEAD_W), lambda bi, g, st, qt, kt, ft: (bi, qt[st], g)),
            pl.BlockSpec((1, tk, hg * Q_HEAD_W), lambda bi, g, st, qt, kt, ft: (bi, kt[st], g)),
            pl.BlockSpec((1, tk, hg * V_HEAD_W), lambda bi, g, st, qt, kt, ft: (bi, kt[st], g)),
        ],
        out_specs=pl.BlockSpec((1, tq, hg * V_HEAD), lambda bi, g, st, qt, kt, ft: (bi, qt[st], g)),
        scratch_shapes=[
            pltpu.VMEM((hg, tq, LANES), F32),
            pltpu.VMEM((hg, tq, V_HEAD_W), F32),
        ],
    )
    return pl.pallas_call(
        kernel,
        grid_spec=grid_spec,
        out_shape=jax.ShapeDtypeStruct((b, t, N_HEADS * V_HEAD), BF16),
        compiler_params=_cparams(("parallel", "parallel", "arbitrary")),
        name="mla_attention",
    )(jnp.asarray(qi_np), jnp.asarray(ki_np), jnp.asarray(fl_np), q, k, v)


def _proj_glu_kernel(xn_ref, wa_ref, wb_ref, wq_ref, glu_ref, qm_ref):
    xn = xn_ref[...]
    a = _dot(xn, wa_ref[...])
    bgate = _dot(xn, wb_ref[...])
    glu_ref[...] = a * jax.nn.sigmoid(bgate)
    qm_ref[...] = _dot(xn, wq_ref[...]).astype(BF16)


def _proj_glu(xn, w5, tm, tn):
    n = xn.shape[0]
    nj = CONV_DIM // tn
    return pl.pallas_call(
        _proj_glu_kernel,
        grid=(nj, n // tm),
        in_specs=[
            pl.BlockSpec((tm, D_MODEL), lambda j, i: (i, 0)),
            pl.BlockSpec((D_MODEL, tn), lambda j, i: (0, j)),
            pl.BlockSpec((D_MODEL, tn), lambda j, i: (0, nj + j)),
            pl.BlockSpec((D_MODEL, tn), lambda j, i: (0, 2 * nj + j)),
        ],
        out_specs=[pl.BlockSpec((tm, tn), lambda j, i: (i, j)),
                   pl.BlockSpec((tm, tn), lambda j, i: (i, j))],
        out_shape=[jax.ShapeDtypeStruct((n, CONV_DIM), F32),
                   jax.ShapeDtypeStruct((n, MEM_DIM), BF16)],
        compiler_params=_cparams(("parallel", "parallel")),
        name="proj_glu_qmem",
    )(xn, w5, w5, w5)


CONV_HALO = 32


SUBLANES = 8
CONV_ROWS = 32


def _conv_kernel(cur_ref, prev_ref, hist_ref, w_ref, b_ref, g_ref, beta_ref, o_ref, buf, shifted, cbuf,
                 wb, *, tt):
    i = pl.program_id(1)
    halo = jnp.where(i == 0, hist_ref[0], prev_ref[0])
    buf[0:CONV_HALO, :] = halo
    buf[CONV_HALO:, :] = cur_ref[0]
    span = tt + CONV_HALO - SUBLANES
    for p in range(1, SUBLANES):
        shifted[p - 1] = buf[p:p + span, :]
    off = CONV_HALO - (CONV_WIDTH - 1)

    for k in range(CONV_WIDTH):
        wb[k] = jnp.broadcast_to(w_ref[k:k + 1, :], (SUBLANES, CONV_DIM))
    wb[CONV_WIDTH] = jnp.broadcast_to(b_ref[...], (SUBLANES, CONV_DIM))

    def chunk(c, carry):
        r0 = pl.multiple_of(c * CONV_ROWS, CONV_ROWS)
        accs = [wb[CONV_WIDTH]] * (CONV_ROWS // SUBLANES)
        for k in range(CONV_WIDTH):
            s = off + k
            p = s % SUBLANES
            wk = wb[k]
            for g in range(CONV_ROWS // SUBLANES):
                if p == 0:
                    rows = buf[pl.ds(r0 + (s + g * SUBLANES), SUBLANES), :]
                else:
                    rows = shifted[p - 1, pl.ds(r0 + (s - p + g * SUBLANES), SUBLANES), :]
                accs[g] = accs[g] + wk * rows
        for g in range(CONV_ROWS // SUBLANES):
            cbuf[pl.ds(r0 + g * SUBLANES, SUBLANES), :] = accs[g]
        return carry

    lax.fori_loop(0, tt // CONV_ROWS, chunk, 0)
    cv = cbuf[...]
    mu = jnp.mean(cv, axis=-1, keepdims=True)
    d = cv - mu
    var = jnp.mean(d * d, axis=-1, keepdims=True)
    y = d * lax.rsqrt(var + EPS) * g_ref[...] + beta_ref[...]
    o_ref[0] = (y * jax.nn.sigmoid(y)).astype(o_ref.dtype)


def _conv_module(glu, hist, w_dw, b_dw, ln_g, ln_b, tt):
    b, t, _ = glu.shape
    r = tt // CONV_HALO
    assert tt % CONV_ROWS == 0
    kernel = functools.partial(_conv_kernel, tt=tt)
    return pl.pallas_call(
        kernel,
        grid=(b, t // tt),
        in_specs=[
            pl.BlockSpec((1, tt, CONV_DIM), lambda bi, i: (bi, i, 0)),
            pl.BlockSpec((1, CONV_HALO, CONV_DIM), lambda bi, i: (bi, jnp.maximum(i * r - 1, 0), 0)),
            pl.BlockSpec((1, CONV_HALO, CONV_DIM), lambda bi, i: (bi, 0, 0)),
            pl.BlockSpec(w_dw.shape, lambda bi, i: (0, 0)),
            pl.BlockSpec((1, CONV_DIM), lambda bi, i: (0, 0)),
            pl.BlockSpec((1, CONV_DIM), lambda bi, i: (0, 0)),
            pl.BlockSpec((1, CONV_DIM), lambda bi, i: (0, 0)),
        ],
        out_specs=pl.BlockSpec((1, tt, CONV_DIM), lambda bi, i: (bi, i, 0)),
        out_shape=jax.ShapeDtypeStruct((b, t, CONV_DIM), BF16),
        scratch_shapes=[pltpu.VMEM((tt + CONV_HALO, CONV_DIM), F32),
                        pltpu.VMEM((SUBLANES - 1, tt + CONV_HALO - SUBLANES, CONV_DIM), F32),
                        pltpu.VMEM((tt, CONV_DIM), F32),
                        pltpu.VMEM((CONV_WIDTH + 1, SUBLANES, CONV_DIM), F32)],
        compiler_params=_cparams(("parallel", "parallel")),
        name="conv_module",
    )(glu, glu, hist, w_dw, b_dw, ln_g, ln_b)


def _mem_attn_kernel(q_ref, k_ref, v_ref, o_ref):
    for h in range(MEM_HEADS):
        sl = slice(h * MEM_HEAD_DIM, (h + 1) * MEM_HEAD_DIM)
        sc = _dot_nt(q_ref[0, :, sl], k_ref[0, :, sl]) * MEM_SCALE
        m = jnp.max(sc, axis=-1, keepdims=True)
        e = jnp.exp(sc - m)
        l = jnp.sum(e, axis=-1, keepdims=True)
        o = _dot(e.astype(BF16), v_ref[0, :, sl])
        o_ref[0, :, sl] = (o / l).astype(o_ref.dtype)


def _mem_attention(qm, mk, mv, tq):
    b, t, _ = qm.shape
    return pl.pallas_call(
        _mem_attn_kernel,
        grid=(b, t // tq),
        in_specs=[
            pl.BlockSpec((1, tq, MEM_DIM), lambda bi, i: (bi, i, 0)),
            pl.BlockSpec((1, MEM_LEN, MEM_DIM), lambda bi, i: (bi, 0, 0)),
            pl.BlockSpec((1, MEM_LEN, MEM_DIM), lambda bi, i: (bi, 0, 0)),
        ],
        out_specs=pl.BlockSpec((1, tq, MEM_DIM), lambda bi, i: (bi, i, 0)),
        out_shape=jax.ShapeDtypeStruct((b, t, MEM_DIM), BF16),
        compiler_params=_cparams(("parallel", "parallel")),
        name="mem_attention",
    )(qm, mk, mv)


def _norm_matmul_kernel(x_ref, g_ref, w_ref, o_ref):
    xn = _rms(x_ref[...], g_ref[...]).astype(BF16)
    o_ref[...] = _dot(xn, w_ref[...])


def _norm_matmul(x, g, w, tm, tn, nout=None):
    n, kdim = x.shape
    nout = w.shape[1] if nout is None else nout
    return pl.pallas_call(
        _norm_matmul_kernel,
        grid=(n // tm, nout // tn),
        in_specs=[
            pl.BlockSpec((tm, kdim), lambda i, j: (i, 0)),
            pl.BlockSpec((1, kdim), lambda i, j: (0, 0)),
            pl.BlockSpec((kdim, tn), lambda i, j: (0, j)),
        ],
        out_specs=pl.BlockSpec((tm, tn), lambda i, j: (i, j)),
        out_shape=jax.ShapeDtypeStruct((n, nout), F32),
        compiler_params=_cparams(("parallel", "arbitrary")),
        name="norm_matmul",
    )(x, g, w)


MERGE_CHAINS = 2


def _merge_kernel(attn_ref, cv_ref, om_ref, xn_ref, x_ref, wa_ref, wb_ref, wc_ref,
                  wg0_ref, wg1_ref, wg2_ref, bg0_ref, bg1_ref, bg2_ref, wo_ref, gffn_ref,
                  h_ref, hn_ref):
    j = pl.program_id(1)

    @pl.when(j == 0)
    def _():
        h_ref[...] = x_ref[...]

    tm = h_ref.shape[0]
    rows = tm // MERGE_CHAINS
    for ch in range(MERGE_CHAINS):
        sl = slice(ch * rows, (ch + 1) * rows)
        xn = xn_ref[sl, :]
        a = _dot(attn_ref[sl, :], wa_ref[...])
        bo = _dot(cv_ref[sl, :], wb_ref[...])
        c = _dot(om_ref[sl, :], wc_ref[...])
        g0 = jax.nn.sigmoid(_dot(xn, wg0_ref[...]) + bg0_ref[...])
        g1 = jax.nn.sigmoid(_dot(xn, wg1_ref[...]) + bg1_ref[...])
        g2 = jax.nn.sigmoid(_dot(xn, wg2_ref[...]) + bg2_ref[...])
        mix = (g0 * a + g1 * bo + g2 * c).astype(BF16)
        h_ref[sl, :] += _dot(mix, wo_ref[...])

    @pl.when(j == pl.num_programs(1) - 1)
    def _():
        hn_ref[...] = _rms(h_ref[...], gffn_ref[...]).astype(BF16)


def _merge(attn, cv, om, xn, x, w_mla_o, w_conv_o, w_mem_o, wg, b_gate, w_out, g_ffn, tm, tn):
    n = x.shape[0]
    nj = D_MODEL // tn
    row = lambda i, j: (i, 0)
    col = lambda i, j: (0, j)
    return pl.pallas_call(
        _merge_kernel,
        grid=(n // tm, nj),
        in_specs=[
            pl.BlockSpec((tm, N_HEADS * V_HEAD), row),
            pl.BlockSpec((tm, CONV_DIM), row),
            pl.BlockSpec((tm, MEM_DIM), row),
            pl.BlockSpec((tm, D_MODEL), row),
            pl.BlockSpec((tm, D_MODEL), row),
            pl.BlockSpec((N_HEADS * V_HEAD, tn), col),
            pl.BlockSpec((CONV_DIM, tn), col),
            pl.BlockSpec((MEM_DIM, tn), col),
            pl.BlockSpec((D_MODEL, tn), lambda i, j: (0, j)),
            pl.BlockSpec((D_MODEL, tn), lambda i, j: (0, nj + j)),
            pl.BlockSpec((D_MODEL, tn), lambda i, j: (0, 2 * nj + j)),
            pl.BlockSpec((1, tn), lambda i, j: (0, j)),
            pl.BlockSpec((1, tn), lambda i, j: (0, nj + j)),
            pl.BlockSpec((1, tn), lambda i, j: (0, 2 * nj + j)),
            pl.BlockSpec((tn, D_MODEL), lambda i, j: (j, 0)),
            pl.BlockSpec((1, D_MODEL), lambda i, j: (0, 0)),
        ],
        out_specs=[pl.BlockSpec((tm, D_MODEL), row), pl.BlockSpec((tm, D_MODEL), row)],
        out_shape=[jax.ShapeDtypeStruct((n, D_MODEL), F32),
                   jax.ShapeDtypeStruct((n, D_MODEL), BF16)],
        compiler_params=_cparams(("parallel", "arbitrary")),
        name="merge_out",
    )(attn, cv, om, xn, x, w_mla_o, w_conv_o, w_mem_o, wg, wg, wg,
      b_gate, b_gate, b_gate, w_out, g_ffn)


FFN_HALO = 16
FFN_CHAINS = 2


def _ffn_kernel(hn_ref, halo_ref, hist_ref, wa_ref, wv_ref, wdw_ref, bdw_ref, wd_ref, h_ref,
                gfin_ref, y_ref, abuf, *, tm, t_len):
    i = pl.program_id(0)
    j = pl.program_id(1)

    @pl.when(j == 0)
    def _():
        y_ref[...] = h_ref[...]

    def conv3(a, b0, rows):
        return (wdw_ref[2:3, :] * a
                + wdw_ref[1:2, :] * abuf[b0 - 1:b0 - 1 + rows, :]
                + wdw_ref[0:1, :] * abuf[b0 - 2:b0 - 2 + rows, :]
                + bdw_ref[...])

    nseq = hist_ref.shape[0]
    if nseq == 1:
        a_prev = _dot(halo_ref[...], wa_ref[...])
        at_seq_start = (i * tm) % t_len == 0
        abuf[0:FFN_HALO, :] = jnp.where(at_seq_start, hist_ref[0], a_prev)
        chains = FFN_CHAINS if tm % (FFN_CHAINS * FFN_HALO) == 0 else 1
        rows = tm // chains
        for ch in range(chains):
            r0 = ch * rows
            hn = hn_ref[r0:r0 + rows, :]
            a = _dot(hn, wa_ref[...])
            val = _dot(hn, wv_ref[...])
            abuf[FFN_HALO + r0:FFN_HALO + r0 + rows, :] = a
            conv = conv3(a, FFN_HALO + r0, rows)
            act = (conv * jax.nn.sigmoid(conv) * val).astype(BF16)
            y_ref[r0:r0 + rows, :] += _dot(act, wd_ref[...])
    else:
        seg = tm // nseq
        hn = hn_ref[...]
        a = _dot(hn, wa_ref[...])
        val = _dot(hn, wv_ref[...])
        convs = []
        for s in range(nseq):
            base = s * (seg + FFN_HALO)
            a_s = a[s * seg:(s + 1) * seg, :]
            abuf[base:base + FFN_HALO, :] = hist_ref[s]
            abuf[base + FFN_HALO:base + FFN_HALO + seg, :] = a_s
            convs.append(conv3(a_s, base + FFN_HALO, seg))
        conv = jnp.concatenate(convs, axis=0)
        act = (conv * jax.nn.sigmoid(conv) * val).astype(BF16)
        y_ref[...] += _dot(act, wd_ref[...])

    @pl.when(j == pl.num_programs(1) - 1)
    def _():
        y_ref[...] = _rms(y_ref[...], gfin_ref[...])


def _ffn(hn, hist, w_up, w_dw, b_dw, w_down, h, g_final, t_len, tm, tn):
    n = hn.shape[0]
    nj = D_FF // tn
    r = tm // FFN_HALO
    nseq = max(1, tm // t_len)
    assert tm % t_len == 0 or t_len % tm == 0
    kernel = functools.partial(_ffn_kernel, tm=tm, t_len=t_len)
    return pl.pallas_call(
        kernel,
        grid=(n // tm, nj),
        in_specs=[
            pl.BlockSpec((tm, D_MODEL), lambda i, j: (i, 0)),
            pl.BlockSpec((FFN_HALO, D_MODEL), lambda i, j: (jnp.maximum(i * r - 1, 0), 0)),
            pl.BlockSpec((nseq, FFN_HALO, tn), lambda i, j: ((i * tm) // (t_len * nseq), 0, j)),
            pl.BlockSpec((D_MODEL, tn), lambda i, j: (0, j)),
            pl.BlockSpec((D_MODEL, tn), lambda i, j: (0, nj + j)),
            pl.BlockSpec((FFN_CONV_WIDTH, tn), lambda i, j: (0, j)),
            pl.BlockSpec((1, tn), lambda i, j: (0, j)),
            pl.BlockSpec((tn, D_MODEL), lambda i, j: (j, 0)),
            pl.BlockSpec((tm, D_MODEL), lambda i, j: (i, 0)),
            pl.BlockSpec((1, D_MODEL), lambda i, j: (0, 0)),
        ],
        out_specs=pl.BlockSpec((tm, D_MODEL), lambda i, j: (i, 0)),
        out_shape=jax.ShapeDtypeStruct((n, D_MODEL), F32),
        scratch_shapes=[pltpu.VMEM((tm + nseq * FFN_HALO, tn), F32)],
        compiler_params=_cparams(("parallel", "arbitrary")),
        name="conv_ffn",
    )(hn, hn, hist, w_up, w_up, w_dw, b_dw, w_down, h, g_final)


def _rope_tables(pos):
    half = QK_ROPE // 2
    inv_freq = ROPE_THETA ** (-jnp.arange(half, dtype=F32) / half)
    ang = pos.astype(F32)[:, None] * inv_freq[None, :]
    cos = jnp.cos(ang)
    sin = jnp.sin(ang)
    cos_t = jnp.tile(cos, (1, LANES // half))
    sin_t = jnp.tile(jnp.concatenate([-sin, sin], axis=1), (1, LANES // QK_ROPE))
    return cos_t, sin_t


def _swap_halves(w):
    half = QK_ROPE // 2
    return jnp.concatenate([w[..., half:], w[..., :half]], axis=-1)


def _prep_weights(w_in, w_uq, w_ukv, w_mla_o, w_conv_o, w_mem_o, w_out, w_up, w_down,
                  w_mem_k, w_mem_v, w_conv_dw):
    kr = w_in[:, OFF_KR:OFF_GLU]
    kr_sw = _swap_halves(kr)
    w1 = jnp.concatenate([w_in[:, :OFF_KR], kr, kr, kr_sw, kr_sw], axis=1).astype(BF16)
    w5 = w_in[:, OFF_GLU:OFF_GATE].astype(BF16)
    wg = w_in[:, OFF_GATE:].astype(BF16)
    nope = w_uq[:, :, :QK_NOPE].reshape(Q_LORA, N_HEADS // 2, 2 * QK_NOPE)
    rope = w_uq[:, :, QK_NOPE:]
    rope_p = rope.reshape(Q_LORA, N_HEADS // 2, 2 * QK_ROPE)
    rope_sw = _swap_halves(rope).reshape(Q_LORA, N_HEADS // 2, 2 * QK_ROPE)
    w2 = jnp.concatenate([nope, rope_p, rope_sw], axis=2).reshape(Q_LORA, -1).astype(BF16)
    w3 = jnp.concatenate([w_ukv[:, :, :QK_NOPE].reshape(KV_LORA, -1),
                          w_ukv[:, :, QK_NOPE:].reshape(KV_LORA, -1)], axis=1).astype(BF16)
    wmem = jnp.concatenate([w_mem_k, w_mem_v], axis=1).astype(BF16)
    wdw = jnp.concatenate([w_conv_dw, jnp.zeros((1, CONV_DIM), F32)], axis=0)
    return dict(w1=w1, w5=w5, wg=wg, w2=w2, w3=w3, wmem=wmem, wdw=wdw,
                w_mla_o=w_mla_o.astype(BF16), w_conv_o=w_conv_o.astype(BF16),
                w_mem_o=w_mem_o.astype(BF16), w_out=w_out.astype(BF16),
                w_up=w_up.astype(BF16), w_down=w_down.astype(BF16))


def _pad_hist(hist, rows):
    b, k, c = hist.shape
    return jnp.concatenate([jnp.zeros((b, rows - k, c), hist.dtype), hist], axis=1)


def _encoder_layer(x, past_len, hist_conv, hist_ffn, past_ckv, past_krope, mem_k, mem_v, wts, prm,
                   cfg):
    b, t, _ = x.shape
    n = b * t
    pos = past_len + jnp.arange(t)
    cos_t, sin_t = _rope_tables(pos)
    tm = cfg["tm"]
    if tm > t:
        cos_t = jnp.tile(cos_t, (tm // t, 1))
        sin_t = jnp.tile(sin_t, (tm // t, 1))
    x2 = x.reshape(n, D_MODEL)

    xn, cqn, ckv, ckv_b, krope, krpad = _proj_lat(
        x2, prm["g_mix"], wts["w1"], prm["g_cq"], prm["g_ckv"], cos_t, sin_t, tm)
    q = _proj_q(cqn, wts["w2"], cos_t, sin_t, tm)

    if past_len:
        pk = past_krope.astype(BF16)
        zeros = jnp.zeros_like(pk)
        past_pad = jnp.concatenate([pk, zeros, zeros, pk], axis=-1)
        ckv_all = jnp.concatenate([past_ckv.astype(BF16), ckv_b.reshape(b, t, KV_LORA)], axis=1)
        kr_all = jnp.concatenate([past_pad, krpad.reshape(b, t, Q_HEAD_W)], axis=1)
    else:
        ckv_all = ckv_b.reshape(b, t, KV_LORA)
        kr_all = krpad.reshape(b, t, Q_HEAD_W)
    s = past_len + t
    kf, vf = _proj_kv(ckv_all.reshape(b * s, KV_LORA), kr_all.reshape(b * s, Q_HEAD_W), wts["w3"],
                      _row_tile(b * s, cfg["tm_kv"]))
    attn = _attention(q.reshape(b, t, -1), kf.reshape(b, s, -1), vf.reshape(b, s, -1),
                      past_len, cfg["tq"], cfg["tk"], cfg["hg"])

    glu, qm = _proj_glu(xn, wts["w5"], cfg["tm_glu"], cfg["tn_glu"])
    glu3 = glu.reshape(b, t, CONV_DIM)
    cv = _conv_module(glu3, _pad_hist(hist_conv, CONV_HALO), wts["wdw"], prm["b_conv_dw"],
                      prm["ln_conv_g"], prm["ln_conv_b"], cfg["tt"])
    om = _mem_attention(qm.reshape(b, t, MEM_DIM), mem_k, mem_v, cfg["tq_mem"])

    h, hn = _merge(attn.reshape(n, -1), cv.reshape(n, CONV_DIM), om.reshape(n, MEM_DIM), xn, x2,
                   wts["w_mla_o"], wts["w_conv_o"], wts["w_mem_o"], wts["wg"], prm["b_gate"],
                   wts["w_out"], prm["g_ffn"], tm, cfg["tn_merge"])
    y = _ffn(hn, _pad_hist(hist_ffn, FFN_HALO), wts["w_up"], prm["w_ffn_dw"], prm["b_ffn_dw"],
             wts["w_down"], h, prm["g_final"], t, cfg["tm_ffn"], cfg["tn_ffn"])

    full_conv = jnp.concatenate([hist_conv, glu3], axis=1)
    new_hist_conv = full_conv[:, full_conv.shape[1] - (CONV_WIDTH - 1):]
    h_tail = h.reshape(b, t, D_MODEL)[:, t - 8:]
    return (y.reshape(b, t, D_MODEL), ckv.reshape(b, t, KV_LORA), krope.reshape(b, t, QK_ROPE),
            new_hist_conv, h_tail)


def kernel(x_prompt, x_sample, mem_prompt, cache_ckv, cache_krope, state_conv, state_ffn_conv,
           cache_mem_k, cache_mem_v, g_mix, w_in, g_cq, w_uq, g_ckv, w_ukv, w_mla_o,
           w_conv_dw, b_conv_dw, ln_conv_g, ln_conv_b, w_conv_o, g_mem, w_mem_k, w_mem_v,
           w_mem_o, b_gate, w_out, g_ffn, w_up, w_ffn_dw, b_ffn_dw, w_down, g_final):
    depth = g_mix.shape[0]
    assert depth == 1
    l = 0
    bp, tp, _ = x_prompt.shape
    bs, ts, _ = x_sample.shape
    past = cache_ckv.shape[2]
    assert tp >= CONV_WIDTH - 1 and ts >= CONV_WIDTH - 1 and ts >= 8

    wts = _prep_weights(w_in[l], w_uq[l], w_ukv[l], w_mla_o[l], w_conv_o[l], w_mem_o[l], w_out[l],
                        w_up[l], w_down[l], w_mem_k[l], w_mem_v[l], w_conv_dw[l])
    row = lambda a: a.reshape(1, -1)
    prm = dict(g_mix=row(g_mix[l]), g_cq=row(g_cq[l]), g_ckv=row(g_ckv[l]),
               b_conv_dw=row(b_conv_dw[l]), ln_conv_g=row(ln_conv_g[l]), ln_conv_b=row(ln_conv_b[l]),
               b_gate=row(b_gate[l]), g_ffn=row(g_ffn[l]), w_ffn_dw=w_ffn_dw[l],
               b_ffn_dw=row(b_ffn_dw[l]), g_final=row(g_final))

    mem2 = mem_prompt.reshape(bp * MEM_LEN, D_MODEL)
    mkv = _norm_matmul(mem2, row(g_mem[l]), wts["wmem"], 256, 512)
    mk_p = mkv[:, :MEM_DIM].reshape(bp, MEM_LEN, MEM_DIM)
    mv_p = mkv[:, MEM_DIM:].reshape(bp, MEM_LEN, MEM_DIM)

    cfg_p = dict(tm=512, tm_kv=512, tq=512, tk=512, hg=16, tm_glu=1024, tn_glu=512, tt=256, tq_mem=512,
                 tn_merge=256, tm_ffn=512, tn_ffn=512)
    n_s = bs * ts
    cfg_s = dict(tm=n_s, tm_kv=512, tq=ts, tk=past + ts, hg=8, tm_glu=n_s, tn_glu=512, tt=ts, tq_mem=ts,
                 tn_merge=256, tm_ffn=n_s, tn_ffn=512)

    zeros_conv = jnp.zeros((bp, CONV_WIDTH - 1, CONV_DIM), F32)
    zeros_ffn = jnp.zeros((bp, FFN_CONV_WIDTH - 1, D_FF), F32)
    yp, ckv_p, kr_p, cs_p, htail_p = _encoder_layer(
        x_prompt, 0, zeros_conv, zeros_ffn, None, None,
        mk_p.astype(BF16), mv_p.astype(BF16), wts, prm, cfg_p)
    ys, ckv_s, kr_s, cs_s, htail_s = _encoder_layer(
        x_sample, past, state_conv[l], state_ffn_conv[l], cache_ckv[l], cache_krope[l],
        cache_mem_k[l].reshape(bs, MEM_LEN, MEM_DIM).astype(BF16),
        cache_mem_v[l].reshape(bs, MEM_LEN, MEM_DIM).astype(BF16), wts, prm, cfg_s)

    tails = jnp.concatenate([htail_p.reshape(bp * 8, D_MODEL), htail_s.reshape(bs * 8, D_MODEL)], axis=0)
    a_tail = _norm_matmul(tails, prm["g_ffn"], wts["w_up"], tails.shape[0], 512, nout=D_FF)
    fs_p = a_tail[:bp * 8].reshape(bp, 8, D_FF)[:, 8 - (FFN_CONV_WIDTH - 1):]
    fs_s = a_tail[bp * 8:].reshape(bs, 8, D_FF)[:, 8 - (FFN_CONV_WIDTH - 1):]

    st = lambda a: a[None]
    return (yp, ys, st(ckv_p), st(kr_p), st(cs_p), st(fs_p),
            st(mk_p.reshape(bp, MEM_LEN, MEM_HEADS, MEM_HEAD_DIM)),
            st(mv_p.reshape(bp, MEM_LEN, MEM_HEADS, MEM_HEAD_DIM)),
            st(ckv_s), st(kr_s), st(cs_s), st(fs_s))
```

```python
import functools

import numpy as np
import jax
import jax.numpy as jnp
from jax import lax
from jax.experimental import pallas as pl
from jax.experimental.pallas import tpu as pltpu

D_MODEL = 2048
CHUNK = 64
EPS = 1e-6
N_HEADS = 16
QK_NOPE = 128
QK_ROPE = 64
V_HEAD = 128
Q_LORA = 512
KV_LORA = 512
ROPE_THETA = 10000.0
MLA_SCALE = (QK_NOPE + QK_ROPE) ** -0.5
CONV_DIM = 1024
CONV_WIDTH = 31
MEM_LEN = 256
MEM_HEADS = 4
MEM_HEAD_DIM = 256
MEM_DIM = MEM_HEADS * MEM_HEAD_DIM
MEM_SCALE = MEM_HEAD_DIM ** -0.5
N_BRANCH = 3
D_FF = 5632
FFN_CONV_WIDTH = 3
OFF_CKV = Q_LORA
OFF_KR = OFF_CKV + KV_LORA
OFF_GLU = OFF_KR + QK_ROPE
OFF_QM = OFF_GLU + 2 * CONV_DIM
OFF_GATE = OFF_QM + MEM_DIM

LANES = 128
V7X_VMEM_LIMIT = 56 * 1024 * 1024
Q_HEAD_W = 2 * LANES
V_HEAD_W = 2 * LANES
NEG_BIG = -1e30
Q_SCALE = MLA_SCALE * float(np.log2(np.e))

BF16 = jnp.bfloat16
F32 = jnp.float32


def _cparams(sem, vmem=V7X_VMEM_LIMIT):
    return pltpu.CompilerParams(dimension_semantics=sem, vmem_limit_bytes=vmem)


def _row_tile(n, cap, mult=16):
    t = min(cap, n)
    t -= t % mult
    while t > mult and n % t:
        t -= mult
    assert t >= mult and n % t == 0, (n, cap)
    return t


def _rms(x, g):
    y = x * lax.rsqrt(jnp.mean(x * x, axis=-1, keepdims=True) + EPS)
    return y * g


def _dot(a, b):
    return jnp.dot(a, b, preferred_element_type=F32)


def _dot_nt(a, b):
    return lax.dot_general(a, b, (((1,), (1,)), ((), ())), preferred_element_type=F32)


def _proj_lat_kernel(x_ref, gmix_ref, w_ref, gcq_ref, gckv_ref, cos_ref, sin_ref,
                     xn_ref, cqn_ref, ckv_ref, ckvb_ref, kr_ref, krpad_ref):
    xn = _rms(x_ref[...], gmix_ref[...]).astype(BF16)
    xn_ref[...] = xn
    z = _dot(xn, w_ref[...])
    cqn_ref[...] = _rms(z[:, :Q_LORA], gcq_ref[...]).astype(BF16)
    ckv = _rms(z[:, OFF_CKV:OFF_KR], gckv_ref[...])
    ckv_ref[...] = ckv
    ckvb_ref[...] = ckv.astype(BF16)
    r = z[:, OFF_KR:OFF_KR + LANES] * cos_ref[...] + z[:, OFF_KR + LANES:] * sin_ref[...]
    kr_ref[...] = r[:, :QK_ROPE]
    lane = lax.broadcasted_iota(jnp.int32, r.shape, 1)
    zero = jnp.zeros_like(r)
    krpad_ref[:, :LANES] = jnp.where(lane < QK_ROPE, r, zero).astype(BF16)
    krpad_ref[:, LANES:] = jnp.where(lane >= QK_ROPE, r, zero).astype(BF16)


def _proj_lat(x, g_mix, w1, g_cq, g_ckv, cos_t, sin_t, tm):
    n = x.shape[0]
    nt = cos_t.shape[0] // tm
    row = lambda i: (i, 0)
    const = lambda i: (0, 0)
    tab = lambda i: (i % nt, 0)
    return pl.pallas_call(
        _proj_lat_kernel,
        grid=(n // tm,),
        in_specs=[
            pl.BlockSpec((tm, D_MODEL), row),
            pl.BlockSpec((1, D_MODEL), const),
            pl.BlockSpec(w1.shape, const),
            pl.BlockSpec((1, Q_LORA), const),
            pl.BlockSpec((1, KV_LORA), const),
            pl.BlockSpec((tm, LANES), tab),
            pl.BlockSpec((tm, LANES), tab),
        ],
        out_specs=[
            pl.BlockSpec((tm, D_MODEL), row),
            pl.BlockSpec((tm, Q_LORA), row),
            pl.BlockSpec((tm, KV_LORA), row),
            pl.BlockSpec((tm, KV_LORA), row),
            pl.BlockSpec((tm, QK_ROPE), row),
            pl.BlockSpec((tm, Q_HEAD_W), row),
        ],
        out_shape=[
            jax.ShapeDtypeStruct((n, D_MODEL), BF16),
            jax.ShapeDtypeStruct((n, Q_LORA), BF16),
            jax.ShapeDtypeStruct((n, KV_LORA), F32),
            jax.ShapeDtypeStruct((n, KV_LORA), BF16),
            jax.ShapeDtypeStruct((n, QK_ROPE), F32),
            jax.ShapeDtypeStruct((n, Q_HEAD_W), BF16),
        ],
        compiler_params=_cparams(("parallel",)),
        name="proj_lat",
    )(x, g_mix, w1, g_cq, g_ckv, cos_t, sin_t)


def _proj_q_kernel(cqn_ref, w_ref, cos_ref, sin_ref, q_ref):
    cqn = cqn_ref[...]
    cos = cos_ref[...]
    sin = sin_ref[...]
    pw = 4 * LANES
    for p in range(N_HEADS // 2):
        r = _dot(cqn, w_ref[:, p * pw:(p + 1) * pw])
        rp = ((r[:, 2 * LANES:3 * LANES] * cos + r[:, 3 * LANES:] * sin) * Q_SCALE).astype(BF16)
        q_ref[:, p * pw:p * pw + LANES] = (r[:, :LANES] * Q_SCALE).astype(BF16)
        q_ref[:, p * pw + LANES:p * pw + 2 * LANES] = rp
        q_ref[:, p * pw + 2 * LANES:p * pw + 3 * LANES] = (r[:, LANES:2 * LANES] * Q_SCALE).astype(BF16)
        q_ref[:, p * pw + 3 * LANES:(p + 1) * pw] = rp


def _proj_q(cqn, w2, cos_t, sin_t, tm):
    n = cqn.shape[0]
    nt = cos_t.shape[0] // tm
    return pl.pallas_call(
        _proj_q_kernel,
        grid=(n // tm,),
        in_specs=[
            pl.BlockSpec((tm, Q_LORA), lambda i: (i, 0)),
            pl.BlockSpec(w2.shape, lambda i: (0, 0)),
            pl.BlockSpec((tm, LANES), lambda i: (i % nt, 0)),
            pl.BlockSpec((tm, LANES), lambda i: (i % nt, 0)),
        ],
        out_specs=pl.BlockSpec((tm, N_HEADS * Q_HEAD_W), lambda i: (i, 0)),
        out_shape=jax.ShapeDtypeStruct((n, N_HEADS * Q_HEAD_W), BF16),
        compiler_params=_cparams(("parallel",)),
        name="proj_q",
    )(cqn, w2, cos_t, sin_t)


def _proj_kv_kernel(c_ref, kr_ref, w_ref, k_ref, v_ref):
    c = c_ref[...]
    hw = N_HEADS * QK_NOPE
    kn = _dot(c, w_ref[:, :hw]).astype(BF16)
    vv = _dot(c, w_ref[:, hw:]).astype(BF16)
    ones = jnp.ones((c.shape[0], LANES), BF16)
    for h in range(N_HEADS):
        par = h % 2
        k_ref[:, h * Q_HEAD_W:h * Q_HEAD_W + QK_NOPE] = kn[:, h * QK_NOPE:(h + 1) * QK_NOPE]
        k_ref[:, h * Q_HEAD_W + QK_NOPE:(h + 1) * Q_HEAD_W] = kr_ref[:, par * LANES:(par + 1) * LANES]
        v_ref[:, h * V_HEAD_W:h * V_HEAD_W + V_HEAD] = vv[:, h * V_HEAD:(h + 1) * V_HEAD]
        v_ref[:, h * V_HEAD_W + V_HEAD:(h + 1) * V_HEAD_W] = ones


def _proj_kv(ckv_b, krpad, w3, tm):
    n = ckv_b.shape[0]
    return pl.pallas_call(
        _proj_kv_kernel,
        grid=(n // tm,),
        in_specs=[
            pl.BlockSpec((tm, KV_LORA), lambda i: (i, 0)),
            pl.BlockSpec((tm, Q_HEAD_W), lambda i: (i, 0)),
            pl.BlockSpec(w3.shape, lambda i: (0, 0)),
        ],
        out_specs=[pl.BlockSpec((tm, N_HEADS * Q_HEAD_W), lambda i: (i, 0)),
                   pl.BlockSpec((tm, N_HEADS * V_HEAD_W), lambda i: (i, 0))],
        out_shape=[jax.ShapeDtypeStruct((n, N_HEADS * Q_HEAD_W), BF16),
                   jax.ShapeDtypeStruct((n, N_HEADS * V_HEAD_W), BF16)],
        compiler_params=_cparams(("parallel",)),
        name="proj_kv",
    )(ckv_b, krpad, w3)


def _attn_schedule(t, s, p, tq, tk):
    qi_l, ki_l, fl_l = [], [], []
    for qi in range(t // tq):
        q_lo = p + qi * tq
        q_hi = p + (qi + 1) * tq - 1
        kmax = min(s, (q_hi // CHUNK + 1) * CHUNK)
        full_vis = (q_lo // CHUNK + 1) * CHUNK
        nk = -(-kmax // tk)
        for ki in range(nk):
            need_mask = (ki + 1) * tk > full_vis
            qi_l.append(qi)
            ki_l.append(ki)
            fl_l.append(int(ki == 0) | (int(ki == nk - 1) << 1) | (int(need_mask) << 2))
    return (np.asarray(qi_l, np.int32), np.asarray(ki_l, np.int32), np.asarray(fl_l, np.int32))


def _attn_kernel(qi_tab, ki_tab, fl_tab, q_ref, k_ref, v_ref, o_ref, m_scr, acc_scr,
                 *, hg, tq, tk, past):
    step = pl.program_id(2)
    flags = fl_tab[step]
    qi = qi_tab[step]
    ki = ki_tab[step]

    @pl.when((flags & 1) != 0)
    def _():
        m_scr[...] = jnp.full(m_scr.shape, NEG_BIG, F32)
        acc_scr[...] = jnp.zeros(acc_scr.shape, F32)

    def body(masked):
        if masked:
            qpos = past + qi * tq + lax.broadcasted_iota(jnp.int32, (tq, tk), 0)
            kpos = ki * tk + lax.broadcasted_iota(jnp.int32, (tq, tk), 1)
            visible = (kpos // CHUNK) <= (qpos // CHUNK)
        for h in range(hg):
            q = q_ref[0, :, h * Q_HEAD_W:(h + 1) * Q_HEAD_W]
            k = k_ref[0, :, h * Q_HEAD_W:(h + 1) * Q_HEAD_W]
            sc = _dot_nt(q, k)
            if masked:
                sc = jnp.where(visible, sc, NEG_BIG)
            m_prev = m_scr[h]
            m_cur = jnp.max(sc, axis=1, keepdims=True)
            m_next = jnp.maximum(m_prev, m_cur)
            if tk % LANES == 0:
                p = jnp.exp2(sc - jnp.concatenate([m_next] * (tk // LANES), axis=1))
            else:
                p = jnp.exp2(sc - m_next[:, :1])
            alpha = jnp.exp2(m_prev - m_next)
            m_scr[h] = m_next
            pv = _dot(p.astype(BF16), v_ref[0, :, h * V_HEAD_W:(h + 1) * V_HEAD_W])
            acc_scr[h] = acc_scr[h] * jnp.concatenate([alpha] * (V_HEAD_W // LANES), axis=1) + pv

    @pl.when((flags & 4) != 0)
    def _():
        body(True)

    @pl.when((flags & 4) == 0)
    def _():
        body(False)

    @pl.when((flags & 2) != 0)
    def _():
        for h in range(hg):
            acc = acc_scr[h]
            o_ref[0, :, h * V_HEAD:(h + 1) * V_HEAD] = (acc[:, :V_HEAD] / acc[:, V_HEAD:]).astype(o_ref.dtype)


def _attention(q, k, v, past, tq, tk, hg):
    b, t, _ = q.shape
    s = k.shape[1]
    qi_np, ki_np, fl_np = _attn_schedule(t, s, past, tq, tk)
    n_steps = int(qi_np.shape[0])
    kernel = functools.partial(_attn_kernel, hg=hg, tq=tq, tk=tk, past=past)
    grid_spec = pltpu.PrefetchScalarGridSpec(
        num_scalar_prefetch=3,
        grid=(b, N_HEADS // hg, n_steps),
        in_specs=[
            pl.BlockSpec((1, tq, hg * Q_HEAD_W), lambda bi, g, st, qt, kt, ft: (bi, qt[st], g)),
            pl.BlockSpec((1, tk, hg * Q_HEAD_W), lambda bi, g, st, qt, kt, ft: (bi, kt[st], g)),
            pl.BlockSpec((1, tk, hg * V_HEAD_W), lambda bi, g, st, qt, kt, ft: (bi, kt[st], g)),
        ],
        out_specs=pl.BlockSpec((1, tq, hg * V_HEAD), lambda bi, g, st, qt, kt, ft: (bi, qt[st], g)),
        scratch_shapes=[
            pltpu.VMEM((hg, tq, LANES), F32),
            pltpu.VMEM((hg, tq, V_HEAD_W), F32),
        ],
    )
    return pl.pallas_call(
        kernel,
        grid_spec=grid_spec,
        out_shape=jax.ShapeDtypeStruct((b, t, N_HEADS * V_HEAD), BF16),
        compiler_params=_cparams(("parallel", "parallel", "arbitrary")),
        name="mla_attention",
    )(jnp.asarray(qi_np), jnp.asarray(ki_np), jnp.asarray(fl_np), q, k, v)


def _proj_glu_kernel(xn_ref, wa_ref, wb_ref, wq_ref, glu_ref, qm_ref):
    xn = xn_ref[...]
    a = _dot(xn, wa_ref[...])
    bgate = _dot(xn, wb_ref[...])
    glu_ref[...] = a * jax.nn.sigmoid(bgate)
    qm_ref[...] = _dot(xn, wq_ref[...]).astype(BF16)


def _proj_glu(xn, w5, tm, tn):
    n = xn.shape[0]
    nj = CONV_DIM // tn
    return pl.pallas_call(
        _proj_glu_kernel,
        grid=(nj, n // tm),
        in_specs=[
            pl.BlockSpec((tm, D_MODEL), lambda j, i: (i, 0)),
            pl.BlockSpec((D_MODEL, tn), lambda j, i: (0, j)),
            pl.BlockSpec((D_MODEL, tn), lambda j, i: (0, nj + j)),
            pl.BlockSpec((D_MODEL, tn), lambda j, i: (0, 2 * nj + j)),
        ],
        out_specs=[pl.BlockSpec((tm, tn), lambda j, i: (i, j)),
                   pl.BlockSpec((tm, tn), lambda j, i: (i, j))],
        out_shape=[jax.ShapeDtypeStruct((n, CONV_DIM), F32),
                   jax.ShapeDtypeStruct((n, MEM_DIM), BF16)],
        compiler_params=_cparams(("parallel", "parallel")),
        name="proj_glu_qmem",
    )(xn, w5, w5, w5)


CONV_HALO = 32


SUBLANES = 8
CONV_ROWS = 32


def _conv_kernel(cur_ref, prev_ref, hist_ref, w_ref, b_ref, g_ref, beta_ref, o_ref, buf, shifted, cbuf,
                 wb, *, tt):
    i = pl.program_id(1)
    halo = jnp.where(i == 0, hist_ref[0], prev_ref[0])
    buf[0:CONV_HALO, :] = halo
    buf[CONV_HALO:, :] = cur_ref[0]
    span = tt + CONV_HALO - SUBLANES
    for p in range(1, SUBLANES):
        shifted[p - 1] = buf[p:p + span, :]
    off = CONV_HALO - (CONV_WIDTH - 1)

    for k in range(CONV_WIDTH):
        wb[k] = jnp.broadcast_to(w_ref[k:k + 1, :], (SUBLANES, CONV_DIM))
    wb[CONV_WIDTH] = jnp.broadcast_to(b_ref[...], (SUBLANES, CONV_DIM))

    def chunk(c, carry):
        r0 = pl.multiple_of(c * CONV_ROWS, CONV_ROWS)
        accs = [wb[CONV_WIDTH]] * (CONV_ROWS // SUBLANES)
        for k in range(CONV_WIDTH):
            s = off + k
            p = s % SUBLANES
            wk = wb[k]
            for g in range(CONV_ROWS // SUBLANES):
                if p == 0:
                    rows = buf[pl.ds(r0 + (s + g * SUBLANES), SUBLANES), :]
                else:
                    rows = shifted[p - 1, pl.ds(r0 + (s - p + g * SUBLANES), SUBLANES), :]
                accs[g] = accs[g] + wk * rows
        for g in range(CONV_ROWS // SUBLANES):
            cbuf[pl.ds(r0 + g * SUBLANES, SUBLANES), :] = accs[g]
        return carry

    lax.fori_loop(0, tt // CONV_ROWS, chunk, 0)
    cv = cbuf[...]
    mu = jnp.mean(cv, axis=-1, keepdims=True)
    d = cv - mu
    var = jnp.mean(d * d, axis=-1, keepdims=True)
    y = d * lax.rsqrt(var + EPS) * g_ref[...] + beta_ref[...]
    o_ref[0] = (y * jax.nn.sigmoid(y)).astype(o_ref.dtype)


def _conv_module(glu, hist, w_dw, b_dw, ln_g, ln_b, tt):
    b, t, _ = glu.shape
    r = tt // CONV_HALO
    assert tt % CONV_ROWS == 0
    kernel = functools.partial(_conv_kernel, tt=tt)
    return pl.pallas_call(
        kernel,
        grid=(b, t // tt),
        in_specs=[
            pl.BlockSpec((1, tt, CONV_DIM), lambda bi, i: (bi, i, 0)),
            pl.BlockSpec((1, CONV_HALO, CONV_DIM), lambda bi, i: (bi, jnp.maximum(i * r - 1, 0), 0)),
            pl.BlockSpec((1, CONV_HALO, CONV_DIM), lambda bi, i: (bi, 0, 0)),
            pl.BlockSpec(w_dw.shape, lambda bi, i: (0, 0)),
            pl.BlockSpec((1, CONV_DIM), lambda bi, i: (0, 0)),
            pl.BlockSpec((1, CONV_DIM), lambda bi, i: (0, 0)),
            pl.BlockSpec((1, CONV_DIM), lambda bi, i: (0, 0)),
        ],
        out_specs=pl.BlockSpec((1, tt, CONV_DIM), lambda bi, i: (bi, i, 0)),
        out_shape=jax.ShapeDtypeStruct((b, t, CONV_DIM), BF16),
        scratch_shapes=[pltpu.VMEM((tt + CONV_HALO, CONV_DIM), F32),
                        pltpu.VMEM((SUBLANES - 1, tt + CONV_HALO - SUBLANES, CONV_DIM), F32),
                        pltpu.VMEM((tt, CONV_DIM), F32),
                        pltpu.VMEM((CONV_WIDTH + 1, SUBLANES, CONV_DIM), F32)],
        compiler_params=_cparams(("parallel", "parallel")),
        name="conv_module",
    )(glu, glu, hist, w_dw, b_dw, ln_g, ln_b)


def _mem_attn_kernel(q_ref, k_ref, v_ref, o_ref):
    for h in range(MEM_HEADS):
        sl = slice(h * MEM_HEAD_DIM, (h + 1) * MEM_HEAD_DIM)
        sc = _dot_nt(q_ref[0, :, sl], k_ref[0, :, sl]) * MEM_SCALE
        m = jnp.max(sc, axis=-1, keepdims=True)
        e = jnp.exp(sc - m)
        l = jnp.sum(e, axis=-1, keepdims=True)
        o = _dot(e.astype(BF16), v_ref[0, :, sl])
        o_ref[0, :, sl] = (o / l).astype(o_ref.dtype)


def _mem_attention(qm, mk, mv, tq):
    b, t, _ = qm.shape
    return pl.pallas_call(
        _mem_attn_kernel,
        grid=(b, t // tq),
        in_specs=[
            pl.BlockSpec((1, tq, MEM_DIM), lambda bi, i: (bi, i, 0)),
            pl.BlockSpec((1, MEM_LEN, MEM_DIM), lambda bi, i: (bi, 0, 0)),
            pl.BlockSpec((1, MEM_LEN, MEM_DIM), lambda bi, i: (bi, 0, 0)),
        ],
        out_specs=pl.BlockSpec((1, tq, MEM_DIM), lambda bi, i: (bi, i, 0)),
        out_shape=jax.ShapeDtypeStruct((b, t, MEM_DIM), BF16),
        compiler_params=_cparams(("parallel", "parallel")),
        name="mem_attention",
    )(qm, mk, mv)


def _norm_matmul_kernel(x_ref, g_ref, w_ref, o_ref):
    xn = _rms(x_ref[...], g_ref[...]).astype(BF16)
    o_ref[...] = _dot(xn, w_ref[...])


def _norm_matmul(x, g, w, tm, tn, nout=None):
    n, kdim = x.shape
    nout = w.shape[1] if nout is None else nout
    return pl.pallas_call(
        _norm_matmul_kernel,
        grid=(n // tm, nout // tn),
        in_specs=[
            pl.BlockSpec((tm, kdim), lambda i, j: (i, 0)),
            pl.BlockSpec((1, kdim), lambda i, j: (0, 0)),
            pl.BlockSpec((kdim, tn), lambda i, j: (0, j)),
        ],
        out_specs=pl.BlockSpec((tm, tn), lambda i, j: (i, j)),
        out_shape=jax.ShapeDtypeStruct((n, nout), F32),
        compiler_params=_cparams(("parallel", "arbitrary")),
        name="norm_matmul",
    )(x, g, w)


MIX_CHAINS = 4


def _mix_kernel(attn_ref, cv_ref, om_ref, xn_ref, wa_ref, wb_ref, wc_ref,
                wg0_ref, wg1_ref, wg2_ref, bg0_ref, bg1_ref, bg2_ref, mix_ref):
    tm = mix_ref.shape[0]
    rows = tm // MIX_CHAINS
    for ch in range(MIX_CHAINS):
        sl = slice(ch * rows, (ch + 1) * rows)
        xn = xn_ref[sl, :]
        a = _dot(attn_ref[sl, :], wa_ref[...])
        bo = _dot(cv_ref[sl, :], wb_ref[...])
        c = _dot(om_ref[sl, :], wc_ref[...])
        g0 = jax.nn.sigmoid(_dot(xn, wg0_ref[...]) + bg0_ref[...])
        g1 = jax.nn.sigmoid(_dot(xn, wg1_ref[...]) + bg1_ref[...])
        g2 = jax.nn.sigmoid(_dot(xn, wg2_ref[...]) + bg2_ref[...])
        mix_ref[sl, :] = (g0 * a + g1 * bo + g2 * c).astype(BF16)


def _branch_mix(attn, cv, om, xn, w_mla_o, w_conv_o, w_mem_o, wg, b_gate, tm, tn):
    n = xn.shape[0]
    nj = D_MODEL // tn
    row = lambda j, i: (i, 0)
    col = lambda j, i: (0, j)
    return pl.pallas_call(
        _mix_kernel,
        grid=(nj, n // tm),
        in_specs=[
            pl.BlockSpec((tm, N_HEADS * V_HEAD), row),
            pl.BlockSpec((tm, CONV_DIM), row),
            pl.BlockSpec((tm, MEM_DIM), row),
            pl.BlockSpec((tm, D_MODEL), row),
            pl.BlockSpec((N_HEADS * V_HEAD, tn), col),
            pl.BlockSpec((CONV_DIM, tn), col),
            pl.BlockSpec((MEM_DIM, tn), col),
            pl.BlockSpec((D_MODEL, tn), lambda j, i: (0, j)),
            pl.BlockSpec((D_MODEL, tn), lambda j, i: (0, nj + j)),
            pl.BlockSpec((D_MODEL, tn), lambda j, i: (0, 2 * nj + j)),
            pl.BlockSpec((1, tn), lambda j, i: (0, j)),
            pl.BlockSpec((1, tn), lambda j, i: (0, nj + j)),
            pl.BlockSpec((1, tn), lambda j, i: (0, 2 * nj + j)),
        ],
        out_specs=pl.BlockSpec((tm, tn), lambda j, i: (i, j)),
        out_shape=jax.ShapeDtypeStruct((n, D_MODEL), BF16),
        compiler_params=_cparams(("parallel", "parallel")),
        name="branch_mix",
    )(attn, cv, om, xn, w_mla_o, w_conv_o, w_mem_o, wg, wg, wg, b_gate, b_gate, b_gate)


def _out_proj_kernel(mix_ref, wo_ref, x_ref, gffn_ref, h_ref, hn_ref):
    h = x_ref[...] + _dot(mix_ref[...], wo_ref[...])
    h_ref[...] = h
    hn_ref[...] = _rms(h, gffn_ref[...]).astype(BF16)


def _out_proj(mix, w_out, x, g_ffn, tm):
    n = x.shape[0]
    row = lambda i: (i, 0)
    return pl.pallas_call(
        _out_proj_kernel,
        grid=(n // tm,),
        in_specs=[
            pl.BlockSpec((tm, D_MODEL), row),
            pl.BlockSpec((D_MODEL, D_MODEL), lambda i: (0, 0), pipeline_mode=pl.Buffered(1)),
            pl.BlockSpec((tm, D_MODEL), row),
            pl.BlockSpec((1, D_MODEL), lambda i: (0, 0)),
        ],
        out_specs=[pl.BlockSpec((tm, D_MODEL), row), pl.BlockSpec((tm, D_MODEL), row)],
        out_shape=[jax.ShapeDtypeStruct((n, D_MODEL), F32),
                   jax.ShapeDtypeStruct((n, D_MODEL), BF16)],
        compiler_params=_cparams(("parallel",)),
        name="out_proj",
    )(mix, w_out, x, g_ffn)


FFN_HALO = 16
FFN_CHAINS = 2


def _ffn_kernel(hn_ref, halo_ref, hist_ref, wa_ref, wv_ref, wdw_ref, bdw_ref, wd_ref, h_ref,
                gfin_ref, y_ref, abuf, *, tm, t_len):
    i = pl.program_id(0)
    j = pl.program_id(1)

    @pl.when(j == 0)
    def _():
        y_ref[...] = h_ref[...]

    def conv3(a, b0, rows):
        return (wdw_ref[2:3, :] * a
                + wdw_ref[1:2, :] * abuf[b0 - 1:b0 - 1 + rows, :]
                + wdw_ref[0:1, :] * abuf[b0 - 2:b0 - 2 + rows, :]
                + bdw_ref[...])

    nseq = hist_ref.shape[0]
    if nseq == 1:
        a_prev = _dot(halo_ref[...], wa_ref[...])
        at_seq_start = (i * tm) % t_len == 0
        abuf[0:FFN_HALO, :] = jnp.where(at_seq_start, hist_ref[0], a_prev)
        chains = FFN_CHAINS if tm % (FFN_CHAINS * FFN_HALO) == 0 else 1
        rows = tm // chains
        for ch in range(chains):
            r0 = ch * rows
            hn = hn_ref[r0:r0 + rows, :]
            a = _dot(hn, wa_ref[...])
            val = _dot(hn, wv_ref[...])
            abuf[FFN_HALO + r0:FFN_HALO + r0 + rows, :] = a
            conv = conv3(a, FFN_HALO + r0, rows)
            act = (conv * jax.nn.sigmoid(conv) * val).astype(BF16)
            y_ref[r0:r0 + rows, :] += _dot(act, wd_ref[...])
    else:
        seg = tm // nseq
        hn = hn_ref[...]
        a = _dot(hn, wa_ref[...])
        val = _dot(hn, wv_ref[...])
        convs = []
        for s in range(nseq):
            base = s * (seg + FFN_HALO)
            a_s = a[s * seg:(s + 1) * seg, :]
            abuf[base:base + FFN_HALO, :] = hist_ref[s]
            abuf[base + FFN_HALO:base + FFN_HALO + seg, :] = a_s
            convs.append(conv3(a_s, base + FFN_HALO, seg))
        conv = jnp.concatenate(convs, axis=0)
        act = (conv * jax.nn.sigmoid(conv) * val).astype(BF16)
        y_ref[...] += _dot(act, wd_ref[...])

    @pl.when(j == pl.num_programs(1) - 1)
    def _():
        y_ref[...] = _rms(y_ref[...], gfin_ref[...])


def _ffn(hn, hist, w_up, w_dw, b_dw, w_down, h, g_final, t_len, tm, tn):
    n = hn.shape[0]
    nj = D_FF // tn
    r = tm // FFN_HALO
    nseq = max(1, tm // t_len)
    assert tm % t_len == 0 or t_len % tm == 0
    kernel = functools.partial(_ffn_kernel, tm=tm, t_len=t_len)
    return pl.pallas_call(
        kernel,
        grid=(n // tm, nj),
        in_specs=[
            pl.BlockSpec((tm, D_MODEL), lambda i, j: (i, 0)),
            pl.BlockSpec((FFN_HALO, D_MODEL), lambda i, j: (jnp.maximum(i * r - 1, 0), 0)),
            pl.BlockSpec((nseq, FFN_HALO, tn), lambda i, j: ((i * tm) // (t_len * nseq), 0, j)),
            pl.BlockSpec((D_MODEL, tn), lambda i, j: (0, j)),
            pl.BlockSpec((D_MODEL, tn), lambda i, j: (0, nj + j)),
            pl.BlockSpec((FFN_CONV_WIDTH, tn), lambda i, j: (0, j)),
            pl.BlockSpec((1, tn), lambda i, j: (0, j)),
            pl.BlockSpec((tn, D_MODEL), lambda i, j: (j, 0)),
            pl.BlockSpec((tm, D_MODEL), lambda i, j: (i, 0), pipeline_mode=pl.Buffered(1)),
            pl.BlockSpec((1, D_MODEL), lambda i, j: (0, 0)),
        ],
        out_specs=pl.BlockSpec((tm, D_MODEL), lambda i, j: (i, 0)),
        out_shape=jax.ShapeDtypeStruct((n, D_MODEL), F32),
        scratch_shapes=[pltpu.VMEM((tm + nseq * FFN_HALO, tn), F32)],
        compiler_params=_cparams(("parallel", "arbitrary")),
        name="conv_ffn",
    )(hn, hn, hist, w_up, w_up, w_dw, b_dw, w_down, h, g_final)


def _rope_tables(pos):
    half = QK_ROPE // 2
    inv_freq = ROPE_THETA ** (-jnp.arange(half, dtype=F32) / half)
    ang = pos.astype(F32)[:, None] * inv_freq[None, :]
    cos = jnp.cos(ang)
    sin = jnp.sin(ang)
    cos_t = jnp.tile(cos, (1, LANES // half))
    sin_t = jnp.tile(jnp.concatenate([-sin, sin], axis=1), (1, LANES // QK_ROPE))
    return cos_t, sin_t


def _swap_halves(w):
    half = QK_ROPE // 2
    return jnp.concatenate([w[..., half:], w[..., :half]], axis=-1)


def _prep_weights(w_in, w_uq, w_ukv, w_mla_o, w_conv_o, w_mem_o, w_out, w_up, w_down,
                  w_mem_k, w_mem_v, w_conv_dw):
    kr = w_in[:, OFF_KR:OFF_GLU]
    kr_sw = _swap_halves(kr)
    w1 = jnp.concatenate([w_in[:, :OFF_KR], kr, kr, kr_sw, kr_sw], axis=1).astype(BF16)
    w5 = w_in[:, OFF_GLU:OFF_GATE].astype(BF16)
    wg = w_in[:, OFF_GATE:].astype(BF16)
    nope = w_uq[:, :, :QK_NOPE].reshape(Q_LORA, N_HEADS // 2, 2 * QK_NOPE)
    rope = w_uq[:, :, QK_NOPE:]
    rope_p = rope.reshape(Q_LORA, N_HEADS // 2, 2 * QK_ROPE)
    rope_sw = _swap_halves(rope).reshape(Q_LORA, N_HEADS // 2, 2 * QK_ROPE)
    w2 = jnp.concatenate([nope, rope_p, rope_sw], axis=2).reshape(Q_LORA, -1).astype(BF16)
    w3 = jnp.concatenate([w_ukv[:, :, :QK_NOPE].reshape(KV_LORA, -1),
                          w_ukv[:, :, QK_NOPE:].reshape(KV_LORA, -1)], axis=1).astype(BF16)
    wmem = jnp.concatenate([w_mem_k, w_mem_v], axis=1).astype(BF16)
    wdw = jnp.concatenate([w_conv_dw, jnp.zeros((1, CONV_DIM), F32)], axis=0)
    return dict(w1=w1, w5=w5, wg=wg, w2=w2, w3=w3, wmem=wmem, wdw=wdw,
                w_mla_o=w_mla_o.astype(BF16), w_conv_o=w_conv_o.astype(BF16),
                w_mem_o=w_mem_o.astype(BF16), w_out=w_out.astype(BF16),
                w_up=w_up.astype(BF16), w_down=w_down.astype(BF16))


def _pad_hist(hist, rows):
    b, k, c = hist.shape
    return jnp.concatenate([jnp.zeros((b, rows - k, c), hist.dtype), hist], axis=1)


def _encoder_layer(x, past_len, hist_conv, hist_ffn, past_ckv, past_krope, mem_k, mem_v, wts, prm,
                   cfg):
    b, t, _ = x.shape
    n = b * t
    pos = past_len + jnp.arange(t)
    cos_t, sin_t = _rope_tables(pos)
    tm = cfg["tm"]
    if tm > t:
        cos_t = jnp.tile(cos_t, (tm // t, 1))
        sin_t = jnp.tile(sin_t, (tm // t, 1))
    x2 = x.reshape(n, D_MODEL)

    xn, cqn, ckv, ckv_b, krope, krpad = _proj_lat(
        x2, prm["g_mix"], wts["w1"], prm["g_cq"], prm["g_ckv"], cos_t, sin_t, tm)
    if past_len:
        pk = past_krope.astype(BF16)
        zeros = jnp.zeros_like(pk)
        past_pad = jnp.concatenate([pk, zeros, zeros, pk], axis=-1)
        ckv_all = jnp.concatenate([past_ckv.astype(BF16), ckv_b.reshape(b, t, KV_LORA)], axis=1)
        kr_all = jnp.concatenate([past_pad, krpad.reshape(b, t, Q_HEAD_W)], axis=1)
    else:
        ckv_all = ckv_b.reshape(b, t, KV_LORA)
        kr_all = krpad.reshape(b, t, Q_HEAD_W)
    s = past_len + t
    q = _proj_q(cqn, wts["w2"], cos_t, sin_t, tm)
    kf, vf = _proj_kv(ckv_all.reshape(b * s, KV_LORA), kr_all.reshape(b * s, Q_HEAD_W), wts["w3"],
                      _row_tile(b * s, cfg["tm_kv"]))
    attn = _attention(q.reshape(b, t, -1), kf.reshape(b, s, -1), vf.reshape(b, s, -1),
                      past_len, cfg["tq"], cfg["tk"], cfg["hg"])

    glu, qm = _proj_glu(xn, wts["w5"], cfg["tm_glu"], cfg["tn_glu"])
    glu3 = glu.reshape(b, t, CONV_DIM)
    cv = _conv_module(glu3, _pad_hist(hist_conv, CONV_HALO), wts["wdw"], prm["b_conv_dw"],
                      prm["ln_conv_g"], prm["ln_conv_b"], cfg["tt"])
    om = _mem_attention(qm.reshape(b, t, MEM_DIM), mem_k, mem_v, cfg["tq_mem"])

    mix = _branch_mix(attn.reshape(n, -1), cv.reshape(n, CONV_DIM), om.reshape(n, MEM_DIM), xn,
                      wts["w_mla_o"], wts["w_conv_o"], wts["w_mem_o"], wts["wg"], prm["b_gate"],
                      cfg["tm_mix"], cfg["tn_mix"])
    h, hn = _out_proj(mix, wts["w_out"], x2, prm["g_ffn"], tm)
    y = _ffn(hn, _pad_hist(hist_ffn, FFN_HALO), wts["w_up"], prm["w_ffn_dw"], prm["b_ffn_dw"],
             wts["w_down"], h, prm["g_final"], t, cfg["tm_ffn"], cfg["tn_ffn"])

    full_conv = jnp.concatenate([hist_conv, glu3], axis=1)
    new_hist_conv = full_conv[:, full_conv.shape[1] - (CONV_WIDTH - 1):]
    h_tail = h.reshape(b, t, D_MODEL)[:, t - 8:]
    return (y.reshape(b, t, D_MODEL), ckv.reshape(b, t, KV_LORA), krope.reshape(b, t, QK_ROPE),
            new_hist_conv, h_tail)


def kernel(x_prompt, x_sample, mem_prompt, cache_ckv, cache_krope, state_conv, state_ffn_conv,
           cache_mem_k, cache_mem_v, g_mix, w_in, g_cq, w_uq, g_ckv, w_ukv, w_mla_o,
           w_conv_dw, b_conv_dw, ln_conv_g, ln_conv_b, w_conv_o, g_mem, w_mem_k, w_mem_v,
           w_mem_o, b_gate, w_out, g_ffn, w_up, w_ffn_dw, b_ffn_dw, w_down, g_final):
    depth = g_mix.shape[0]
    assert depth == 1
    l = 0
    bp, tp, _ = x_prompt.shape
    bs, ts, _ = x_sample.shape
    past = cache_ckv.shape[2]
    assert tp >= CONV_WIDTH - 1 and ts >= CONV_WIDTH - 1 and ts >= 8

    wts = _prep_weights(w_in[l], w_uq[l], w_ukv[l], w_mla_o[l], w_conv_o[l], w_mem_o[l], w_out[l],
                        w_up[l], w_down[l], w_mem_k[l], w_mem_v[l], w_conv_dw[l])
    row = lambda a: a.reshape(1, -1)
    prm = dict(g_mix=row(g_mix[l]), g_cq=row(g_cq[l]), g_ckv=row(g_ckv[l]),
               b_conv_dw=row(b_conv_dw[l]), ln_conv_g=row(ln_conv_g[l]), ln_conv_b=row(ln_conv_b[l]),
               b_gate=row(b_gate[l]), g_ffn=row(g_ffn[l]), w_ffn_dw=w_ffn_dw[l],
               b_ffn_dw=row(b_ffn_dw[l]), g_final=row(g_final))

    mem2 = mem_prompt.reshape(bp * MEM_LEN, D_MODEL)
    mkv = _norm_matmul(mem2, row(g_mem[l]), wts["wmem"], 256, 512)
    mk_p = mkv[:, :MEM_DIM].reshape(bp, MEM_LEN, MEM_DIM)
    mv_p = mkv[:, MEM_DIM:].reshape(bp, MEM_LEN, MEM_DIM)

    cfg_p = dict(tm=512, tm_kv=512, tq=512, tk=512, hg=16, tm_glu=1024, tn_glu=512, tt=256, tq_mem=512,
                 tm_mix=1024, tn_mix=256, tm_ffn=1024, tn_ffn=512)
    n_s = bs * ts
    cfg_s = dict(tm=n_s, tm_kv=512, tq=ts, tk=past + ts, hg=8, tm_glu=n_s, tn_glu=512, tt=ts, tq_mem=ts,
                 tm_mix=n_s, tn_mix=256, tm_ffn=n_s, tn_ffn=512)

    zeros_conv = jnp.zeros((bp, CONV_WIDTH - 1, CONV_DIM), F32)
    zeros_ffn = jnp.zeros((bp, FFN_CONV_WIDTH - 1, D_FF), F32)
    yp, ckv_p, kr_p, cs_p, htail_p = _encoder_layer(
        x_prompt, 0, zeros_conv, zeros_ffn, None, None,
        mk_p.astype(BF16), mv_p.astype(BF16), wts, prm, cfg_p)
    ys, ckv_s, kr_s, cs_s, htail_s = _encoder_layer(
        x_sample, past, state_conv[l], state_ffn_conv[l], cache_ckv[l], cache_krope[l],
        cache_mem_k[l].reshape(bs, MEM_LEN, MEM_DIM).astype(BF16),
        cache_mem_v[l].reshape(bs, MEM_LEN, MEM_DIM).astype(BF16), wts, prm, cfg_s)

    tails = jnp.concatenate([htail_p.reshape(bp * 8, D_MODEL), htail_s.reshape(bs * 8, D_MODEL)], axis=0)
    a_tail = _norm_matmul(tails, prm["g_ffn"], wts["w_up"], tails.shape[0], 512, nout=D_FF)
    fs_p = a_tail[:bp * 8].reshape(bp, 8, D_FF)[:, 8 - (FFN_CONV_WIDTH - 1):]
    fs_s = a_tail[bp * 8:].reshape(bs, 8, D_FF)[:, 8 - (FFN_CONV_WIDTH - 1):]

    st = lambda a: a[None]
    return (yp, ys, st(ckv_p), st(kr_p), st(cs_p), st(fs_p),
            st(mk_p.reshape(bp, MEM_LEN, MEM_HEADS, MEM_HEAD_DIM)),
            st(mv_p.reshape(bp, MEM_LEN, MEM_HEADS, MEM_HEAD_DIM)),
            st(ckv_s), st(kr_s), st(cs_s), st(fs_s))
```

```python
import functools

import numpy as np
import jax
import jax.numpy as jnp
from jax import lax
from jax.experimental import pallas as pl
from jax.experimental.pallas import tpu as pltpu

D_MODEL = 2048
CHUNK = 64
EPS = 1e-6
N_HEADS = 16
QK_NOPE = 128
QK_ROPE = 64
V_HEAD = 128
Q_LORA = 512
KV_LORA = 512
ROPE_THETA = 10000.0
MLA_SCALE = (QK_NOPE + QK_ROPE) ** -0.5
CONV_DIM = 1024
CONV_WIDTH = 31
MEM_LEN = 256
MEM_HEADS = 4
MEM_HEAD_DIM = 256
MEM_DIM = MEM_HEADS * MEM_HEAD_DIM
MEM_SCALE = MEM_HEAD_DIM ** -0.5
N_BRANCH = 3
D_FF = 5632
FFN_CONV_WIDTH = 3
OFF_CKV = Q_LORA
OFF_KR = OFF_CKV + KV_LORA
OFF_GLU = OFF_KR + QK_ROPE
OFF_QM = OFF_GLU + 2 * CONV_DIM
OFF_GATE = OFF_QM + MEM_DIM

LANES = 128
V7X_VMEM_LIMIT = 56 * 1024 * 1024
Q_HEAD_W = 2 * LANES
V_HEAD_W = 2 * LANES
NEG_BIG = -1e30
Q_SCALE = MLA_SCALE * float(np.log2(np.e))

BF16 = jnp.bfloat16
F32 = jnp.float32


def _cparams(sem, vmem=V7X_VMEM_LIMIT):
    return pltpu.CompilerParams(dimension_semantics=sem, vmem_limit_bytes=vmem)


def _row_tile(n, cap, mult=16):
    t = min(cap, n)
    t -= t % mult
    while t > mult and n % t:
        t -= mult
    assert t >= mult and n % t == 0, (n, cap)
    return t


def _rms(x, g):
    y = x * lax.rsqrt(jnp.mean(x * x, axis=-1, keepdims=True) + EPS)
    return y * g


def _dot(a, b):
    return jnp.dot(a, b, preferred_element_type=F32)


def _dot_nt(a, b):
    return lax.dot_general(a, b, (((1,), (1,)), ((), ())), preferred_element_type=F32)


def _proj_lat_kernel(x_ref, gmix_ref, w_ref, gcq_ref, gckv_ref, cos_ref, sin_ref,
                     xn_ref, cqn_ref, ckv_ref, ckvb_ref, kr_ref, krpad_ref):
    xn = _rms(x_ref[...], gmix_ref[...]).astype(BF16)
    xn_ref[...] = xn
    z = _dot(xn, w_ref[...])
    cqn_ref[...] = _rms(z[:, :Q_LORA], gcq_ref[...]).astype(BF16)
    ckv = _rms(z[:, OFF_CKV:OFF_KR], gckv_ref[...])
    ckv_ref[...] = ckv
    ckvb_ref[...] = ckv.astype(BF16)
    r = z[:, OFF_KR:OFF_KR + LANES] * cos_ref[...] + z[:, OFF_KR + LANES:] * sin_ref[...]
    kr_ref[...] = r[:, :QK_ROPE]
    lane = lax.broadcasted_iota(jnp.int32, r.shape, 1)
    zero = jnp.zeros_like(r)
    krpad_ref[:, :LANES] = jnp.where(lane < QK_ROPE, r, zero).astype(BF16)
    krpad_ref[:, LANES:] = jnp.where(lane >= QK_ROPE, r, zero).astype(BF16)


def _proj_lat(x, g_mix, w1, g_cq, g_ckv, cos_t, sin_t, tm):
    n = x.shape[0]
    nt = cos_t.shape[0] // tm
    row = lambda i: (i, 0)
    const = lambda i: (0, 0)
    tab = lambda i: (i % nt, 0)
    return pl.pallas_call(
        _proj_lat_kernel,
        grid=(n // tm,),
        in_specs=[
            pl.BlockSpec((tm, D_MODEL), row),
            pl.BlockSpec((1, D_MODEL), const),
            pl.BlockSpec(w1.shape, const),
            pl.BlockSpec((1, Q_LORA), const),
            pl.BlockSpec((1, KV_LORA), const),
            pl.BlockSpec((tm, LANES), tab),
            pl.BlockSpec((tm, LANES), tab),
        ],
        out_specs=[
            pl.BlockSpec((tm, D_MODEL), row),
            pl.BlockSpec((tm, Q_LORA), row),
            pl.BlockSpec((tm, KV_LORA), row),
            pl.BlockSpec((tm, KV_LORA), row),
            pl.BlockSpec((tm, QK_ROPE), row),
            pl.BlockSpec((tm, Q_HEAD_W), row),
        ],
        out_shape=[
            jax.ShapeDtypeStruct((n, D_MODEL), BF16),
            jax.ShapeDtypeStruct((n, Q_LORA), BF16),
            jax.ShapeDtypeStruct((n, KV_LORA), F32),
            jax.ShapeDtypeStruct((n, KV_LORA), BF16),
            jax.ShapeDtypeStruct((n, QK_ROPE), F32),
            jax.ShapeDtypeStruct((n, Q_HEAD_W), BF16),
        ],
        compiler_params=_cparams(("parallel",)),
        name="proj_lat",
    )(x, g_mix, w1, g_cq, g_ckv, cos_t, sin_t)


def _proj_q_kernel(cqn_ref, w_ref, cos_ref, sin_ref, q_ref):
    cqn = cqn_ref[...]
    cos = cos_ref[...]
    sin = sin_ref[...]
    pw = 4 * LANES
    for p in range(N_HEADS // 2):
        r = _dot(cqn, w_ref[:, p * pw:(p + 1) * pw])
        rp = ((r[:, 2 * LANES:3 * LANES] * cos + r[:, 3 * LANES:] * sin) * Q_SCALE).astype(BF16)
        q_ref[:, p * pw:p * pw + LANES] = (r[:, :LANES] * Q_SCALE).astype(BF16)
        q_ref[:, p * pw + LANES:p * pw + 2 * LANES] = rp
        q_ref[:, p * pw + 2 * LANES:p * pw + 3 * LANES] = (r[:, LANES:2 * LANES] * Q_SCALE).astype(BF16)
        q_ref[:, p * pw + 3 * LANES:(p + 1) * pw] = rp


def _proj_q(cqn, w2, cos_t, sin_t, tm):
    n = cqn.shape[0]
    nt = cos_t.shape[0] // tm
    return pl.pallas_call(
        _proj_q_kernel,
        grid=(n // tm,),
        in_specs=[
            pl.BlockSpec((tm, Q_LORA), lambda i: (i, 0)),
            pl.BlockSpec(w2.shape, lambda i: (0, 0)),
            pl.BlockSpec((tm, LANES), lambda i: (i % nt, 0)),
            pl.BlockSpec((tm, LANES), lambda i: (i % nt, 0)),
        ],
        out_specs=pl.BlockSpec((tm, N_HEADS * Q_HEAD_W), lambda i: (i, 0)),
        out_shape=jax.ShapeDtypeStruct((n, N_HEADS * Q_HEAD_W), BF16),
        compiler_params=_cparams(("parallel",)),
        name="proj_q",
    )(cqn, w2, cos_t, sin_t)


def _proj_kv_kernel(c_ref, kr_ref, w_ref, k_ref, v_ref):
    c = c_ref[...]
    hw = N_HEADS * QK_NOPE
    kn = _dot(c, w_ref[:, :hw]).astype(BF16)
    vv = _dot(c, w_ref[:, hw:]).astype(BF16)
    ones = jnp.ones((c.shape[0], LANES), BF16)
    for h in range(N_HEADS):
        par = h % 2
        k_ref[:, h * Q_HEAD_W:h * Q_HEAD_W + QK_NOPE] = kn[:, h * QK_NOPE:(h + 1) * QK_NOPE]
        k_ref[:, h * Q_HEAD_W + QK_NOPE:(h + 1) * Q_HEAD_W] = kr_ref[:, par * LANES:(par + 1) * LANES]
        v_ref[:, h * V_HEAD_W:h * V_HEAD_W + V_HEAD] = vv[:, h * V_HEAD:(h + 1) * V_HEAD]
        v_ref[:, h * V_HEAD_W + V_HEAD:(h + 1) * V_HEAD_W] = ones


def _proj_kv(ckv_b, krpad, w3, tm):
    n = ckv_b.shape[0]
    return pl.pallas_call(
        _proj_kv_kernel,
        grid=(n // tm,),
        in_specs=[
            pl.BlockSpec((tm, KV_LORA), lambda i: (i, 0)),
            pl.BlockSpec((tm, Q_HEAD_W), lambda i: (i, 0)),
            pl.BlockSpec(w3.shape, lambda i: (0, 0)),
        ],
        out_specs=[pl.BlockSpec((tm, N_HEADS * Q_HEAD_W), lambda i: (i, 0)),
                   pl.BlockSpec((tm, N_HEADS * V_HEAD_W), lambda i: (i, 0))],
        out_shape=[jax.ShapeDtypeStruct((n, N_HEADS * Q_HEAD_W), BF16),
                   jax.ShapeDtypeStruct((n, N_HEADS * V_HEAD_W), BF16)],
        compiler_params=_cparams(("parallel",)),
        name="proj_kv",
    )(ckv_b, krpad, w3)


V_ROWS = V_HEAD + 16


def _proj_kvt_kernel(c_ref, kr_ref, wk_ref, wvt_ref, k_ref, vt_ref):
    c = c_ref[...]
    kn = _dot(c, wk_ref[...]).astype(BF16)
    for h in range(N_HEADS):
        par = h % 2
        k_ref[:, h * Q_HEAD_W:h * Q_HEAD_W + QK_NOPE] = kn[:, h * QK_NOPE:(h + 1) * QK_NOPE]
        k_ref[:, h * Q_HEAD_W + QK_NOPE:(h + 1) * Q_HEAD_W] = kr_ref[:, par * LANES:(par + 1) * LANES]
    vvt = _dot_nt(wvt_ref[...], c).astype(BF16)
    ones = jnp.ones((V_ROWS - V_HEAD, c.shape[0]), BF16)
    for h in range(N_HEADS):
        vt_ref[0, h * V_ROWS:h * V_ROWS + V_HEAD, :] = vvt[h * V_HEAD:(h + 1) * V_HEAD, :]
        vt_ref[0, h * V_ROWS + V_HEAD:(h + 1) * V_ROWS, :] = ones


def _proj_kvt(ckv_b, krpad, wk, wvt, b, s, tm):
    nt = s // tm
    return pl.pallas_call(
        _proj_kvt_kernel,
        grid=(b * nt,),
        in_specs=[
            pl.BlockSpec((tm, KV_LORA), lambda i: (i, 0)),
            pl.BlockSpec((tm, Q_HEAD_W), lambda i: (i, 0)),
            pl.BlockSpec(wk.shape, lambda i: (0, 0)),
            pl.BlockSpec(wvt.shape, lambda i: (0, 0)),
        ],
        out_specs=[pl.BlockSpec((tm, N_HEADS * Q_HEAD_W), lambda i: (i, 0)),
                   pl.BlockSpec((1, N_HEADS * V_ROWS, tm), lambda i: (i // nt, 0, i % nt))],
        out_shape=[jax.ShapeDtypeStruct((b * s, N_HEADS * Q_HEAD_W), BF16),
                   jax.ShapeDtypeStruct((b, N_HEADS * V_ROWS, s), BF16)],
        compiler_params=_cparams(("parallel",)),
        name="proj_kvt",
    )(ckv_b, krpad, wk, wvt)


def _attn_schedule(t, s, p, tq, tk):
    qi_l, ki_l, fl_l = [], [], []
    for qi in range(t // tq):
        q_lo = p + qi * tq
        q_hi = p + (qi + 1) * tq - 1
        kmax = min(s, (q_hi // CHUNK + 1) * CHUNK)
        full_vis = (q_lo // CHUNK + 1) * CHUNK
        nk = -(-kmax // tk)
        for ki in range(nk):
            need_mask = (ki + 1) * tk > full_vis
            qi_l.append(qi)
            ki_l.append(ki)
            fl_l.append(int(ki == 0) | (int(ki == nk - 1) << 1) | (int(need_mask) << 2))
    return (np.asarray(qi_l, np.int32), np.asarray(ki_l, np.int32), np.asarray(fl_l, np.int32))


def _attn_kernel(qi_tab, ki_tab, fl_tab, q_ref, k_ref, v_ref, o_ref, m_scr, acc_scr,
                 *, hg, tq, tk, past):
    step = pl.program_id(2)
    flags = fl_tab[step]
    qi = qi_tab[step]
    ki = ki_tab[step]

    @pl.when((flags & 1) != 0)
    def _():
        m_scr[...] = jnp.full(m_scr.shape, NEG_BIG, F32)
        acc_scr[...] = jnp.zeros(acc_scr.shape, F32)

    def body(masked):
        if masked:
            qpos = past + qi * tq + lax.broadcasted_iota(jnp.int32, (tq, tk), 0)
            kpos = ki * tk + lax.broadcasted_iota(jnp.int32, (tq, tk), 1)
            visible = (kpos // CHUNK) <= (qpos // CHUNK)
        for h in range(hg):
            q = q_ref[0, :, h * Q_HEAD_W:(h + 1) * Q_HEAD_W]
            k = k_ref[0, :, h * Q_HEAD_W:(h + 1) * Q_HEAD_W]
            sc = _dot_nt(q, k)
            if masked:
                sc = jnp.where(visible, sc, NEG_BIG)
            m_prev = m_scr[h]
            m_cur = jnp.max(sc, axis=1, keepdims=True)
            m_next = jnp.maximum(m_prev, m_cur)
            if tk % LANES == 0:
                p = jnp.exp2(sc - jnp.concatenate([m_next] * (tk // LANES), axis=1))
            else:
                p = jnp.exp2(sc - m_next[:, :1])
            alpha = jnp.exp2(m_prev - m_next)
            m_scr[h] = m_next
            pv = _dot(p.astype(BF16), v_ref[0, :, h * V_HEAD_W:(h + 1) * V_HEAD_W])
            acc_scr[h] = acc_scr[h] * jnp.concatenate([alpha] * (V_HEAD_W // LANES), axis=1) + pv

    @pl.when((flags & 4) != 0)
    def _():
        body(True)

    @pl.when((flags & 4) == 0)
    def _():
        body(False)

    @pl.when((flags & 2) != 0)
    def _():
        for h in range(hg):
            acc = acc_scr[h]
            o_ref[0, :, h * V_HEAD:(h + 1) * V_HEAD] = (acc[:, :V_HEAD] / acc[:, V_HEAD:]).astype(o_ref.dtype)


def _attention(q, k, v, past, tq, tk, hg):
    b, t, _ = q.shape
    s = k.shape[1]
    qi_np, ki_np, fl_np = _attn_schedule(t, s, past, tq, tk)
    n_steps = int(qi_np.shape[0])
    kernel = functools.partial(_attn_kernel, hg=hg, tq=tq, tk=tk, past=past)
    grid_spec = pltpu.PrefetchScalarGridSpec(
        num_scalar_prefetch=3,
        grid=(b, N_HEADS // hg, n_steps),
        in_specs=[
            pl.BlockSpec((1, tq, hg * Q_HEAD_W), lambda bi, g, st, qt, kt, ft: (bi, qt[st], g)),
            pl.BlockSpec((1, tk, hg * Q_HEAD_W), lambda bi, g, st, qt, kt, ft: (bi, kt[st], g)),
            pl.BlockSpec((1, tk, hg * V_HEAD_W), lambda bi, g, st, qt, kt, ft: (bi, kt[st], g)),
        ],
        out_specs=pl.BlockSpec((1, tq, hg * V_HEAD), lambda bi, g, st, qt, kt, ft: (bi, qt[st], g)),
        scratch_shapes=[
            pltpu.VMEM((hg, tq, LANES), F32),
            pltpu.VMEM((hg, tq, V_HEAD_W), F32),
        ],
    )
    return pl.pallas_call(
        kernel,
        grid_spec=grid_spec,
        out_shape=jax.ShapeDtypeStruct((b, t, N_HEADS * V_HEAD), BF16),
        compiler_params=_cparams(("parallel", "parallel", "arbitrary")),
        name="mla_attention",
    )(jnp.asarray(qi_np), jnp.asarray(ki_np), jnp.asarray(fl_np), q, k, v)


def _attn_vt_kernel(qi_tab, ki_tab, fl_tab, q_ref, k_ref, vt_ref, o_ref, m_scr, acc_scr,
                    *, hg, tq, tk, past):
    step = pl.program_id(2)
    flags = fl_tab[step]
    qi = qi_tab[step]
    ki = ki_tab[step]

    @pl.when((flags & 1) != 0)
    def _():
        m_scr[...] = jnp.full(m_scr.shape, NEG_BIG, F32)
        acc_scr[...] = jnp.zeros(acc_scr.shape, F32)

    def body(masked):
        if masked:
            qpos = past + qi * tq + lax.broadcasted_iota(jnp.int32, (tq, tk), 0)
            kpos = ki * tk + lax.broadcasted_iota(jnp.int32, (tq, tk), 1)
            visible = (kpos // CHUNK) <= (qpos // CHUNK)
        def softmax(h):
            q = q_ref[0, :, h * Q_HEAD_W:(h + 1) * Q_HEAD_W]
            k = k_ref[0, :, h * Q_HEAD_W:(h + 1) * Q_HEAD_W]
            sc = _dot_nt(q, k)
            if masked:
                sc = jnp.where(visible, sc, NEG_BIG)
            m_prev = m_scr[h]
            m_next = jnp.maximum(m_prev, jnp.max(sc, axis=1, keepdims=True))
            p = jnp.exp2(sc - jnp.concatenate([m_next] * (tk // LANES), axis=1))
            alpha_t = jnp.exp2(m_prev - m_next).T
            m_scr[h] = m_next
            return p.astype(BF16), alpha_t

        nxt = softmax(0)
        for h in range(hg):
            p, alpha_t = nxt
            if h + 1 < hg:
                nxt = softmax(h + 1)
            pvt = _dot_nt(vt_ref[0, h * V_ROWS:(h + 1) * V_ROWS, :], p)
            scale = jnp.concatenate([alpha_t, alpha_t[:V_ROWS - V_HEAD]], axis=0)
            acc_scr[h] = acc_scr[h] * scale + pvt

    @pl.when((flags & 4) != 0)
    def _():
        body(True)

    @pl.when((flags & 4) == 0)
    def _():
        body(False)

    @pl.when((flags & 2) != 0)
    def _():
        for h in range(hg):
            acc = acc_scr[h]
            o = acc[:V_HEAD] / acc[V_HEAD:V_HEAD + 1]
            o_ref[0, :, h * V_HEAD:(h + 1) * V_HEAD] = o.T.astype(o_ref.dtype)


def _attention_vt(q, k, vt, past, tq, tk, hg):
    b, t, _ = q.shape
    s = k.shape[1]
    assert tq % LANES == 0 and tk % LANES == 0
    qi_np, ki_np, fl_np = _attn_schedule(t, s, past, tq, tk)
    n_steps = int(qi_np.shape[0])
    kernel = functools.partial(_attn_vt_kernel, hg=hg, tq=tq, tk=tk, past=past)
    grid_spec = pltpu.PrefetchScalarGridSpec(
        num_scalar_prefetch=3,
        grid=(b, N_HEADS // hg, n_steps),
        in_specs=[
            pl.BlockSpec((1, tq, hg * Q_HEAD_W), lambda bi, g, st, qt, kt, ft: (bi, qt[st], g)),
            pl.BlockSpec((1, tk, hg * Q_HEAD_W), lambda bi, g, st, qt, kt, ft: (bi, kt[st], g)),
            pl.BlockSpec((1, hg * V_ROWS, tk), lambda bi, g, st, qt, kt, ft: (bi, g, kt[st])),
        ],
        out_specs=pl.BlockSpec((1, tq, hg * V_HEAD), lambda bi, g, st, qt, kt, ft: (bi, qt[st], g)),
        scratch_shapes=[
            pltpu.VMEM((hg, tq, LANES), F32),
            pltpu.VMEM((hg, V_ROWS, tq), F32),
        ],
    )
    return pl.pallas_call(
        kernel,
        grid_spec=grid_spec,
        out_shape=jax.ShapeDtypeStruct((b, t, N_HEADS * V_HEAD), BF16),
        compiler_params=_cparams(("parallel", "parallel", "arbitrary")),
        name="mla_attention_vt",
    )(jnp.asarray(qi_np), jnp.asarray(ki_np), jnp.asarray(fl_np), q, k, vt)


CONV_HALO = 32
SUBLANES = 8
CONV_ROWS = 32
GLU_CHAINS = 2


def _glu_conv_kernel(xn_ref, w_ref, hist_ref, wdw_ref, b_ref, g_ref, beta_ref,
                     qm_ref, cv_ref, tail_ref, carry, wb, *chain_scratch, tm, t_len):
    i = pl.program_id(0)
    gbufs, shifteds, cbufs = chain_scratch[0::3], chain_scratch[1::3], chain_scratch[2::3]
    chains = len(gbufs)
    rows = tm // chains

    @pl.when((i * tm) % t_len == 0)
    def _():
        carry[...] = hist_ref[0]

    for k in range(CONV_WIDTH):
        wb[k] = jnp.broadcast_to(wdw_ref[k:k + 1, :], (SUBLANES, CONV_DIM))
    wb[CONV_WIDTH] = jnp.broadcast_to(b_ref[...], (SUBLANES, CONV_DIM))

    off = CONV_HALO - (CONV_WIDTH - 1)
    groups = CONV_ROWS // SUBLANES
    for ch in range(chains):
        r0 = ch * rows
        gbuf, shifted, cbuf = gbufs[ch], shifteds[ch], cbufs[ch]
        gbuf[0:CONV_HALO, :] = carry[...] if ch == 0 else gbufs[ch - 1][rows:rows + CONV_HALO, :]
        xn = xn_ref[r0:r0 + rows, :]
        a = _dot(xn, w_ref[:, :CONV_DIM])
        bgate = _dot(xn, w_ref[:, CONV_DIM:2 * CONV_DIM])
        gbuf[CONV_HALO:, :] = a * jax.nn.sigmoid(bgate)
        qm_ref[r0:r0 + rows, :] = _dot(xn, w_ref[:, 2 * CONV_DIM:]).astype(BF16)
        span = rows + CONV_HALO - SUBLANES
        for p in range(1, SUBLANES):
            shifted[p - 1] = gbuf[p:p + span, :]
        for c0 in range(0, rows, CONV_ROWS):
            accs = [wb[CONV_WIDTH]] * groups
            for k in range(CONV_WIDTH):
                s = off + k + c0
                p = s % SUBLANES
                wk = wb[k]
                for g in range(groups):
                    if p == 0:
                        rws = gbuf[s + g * SUBLANES:s + (g + 1) * SUBLANES, :]
                    else:
                        rws = shifted[p - 1, s - p + g * SUBLANES:s - p + (g + 1) * SUBLANES, :]
                    accs[g] = accs[g] + wk * rws
            for g in range(groups):
                cbuf[c0 + g * SUBLANES:c0 + (g + 1) * SUBLANES, :] = accs[g]
        cv = cbuf[...]
        mu = jnp.mean(cv, axis=-1, keepdims=True)
        d = cv - mu
        var = jnp.mean(d * d, axis=-1, keepdims=True)
        y = d * lax.rsqrt(var + EPS) * g_ref[...] + beta_ref[...]
        cv_ref[r0:r0 + rows, :] = (y * jax.nn.sigmoid(y)).astype(cv_ref.dtype)

    last = gbufs[-1][rows:rows + CONV_HALO, :]
    tail_ref[0] = last
    carry[...] = last


def _glu_conv(xn, w5, hist, w_dw, b_dw, ln_g, ln_b, t_len, tm):
    n = xn.shape[0]
    assert t_len % tm == 0 and tm % CONV_ROWS == 0 and tm >= CONV_HALO
    nt = t_len // tm
    chains = GLU_CHAINS if tm % (GLU_CHAINS * CONV_ROWS) == 0 else 1
    rows = tm // chains
    chain_scratch = [pltpu.VMEM((rows + CONV_HALO, CONV_DIM), F32),
                     pltpu.VMEM((SUBLANES - 1, rows + CONV_HALO - SUBLANES, CONV_DIM), F32),
                     pltpu.VMEM((rows, CONV_DIM), F32)] * chains
    kernel = functools.partial(_glu_conv_kernel, tm=tm, t_len=t_len)
    const = lambda i: (0, 0)
    return pl.pallas_call(
        kernel,
        grid=(n // tm,),
        in_specs=[
            pl.BlockSpec((tm, D_MODEL), lambda i: (i, 0)),
            pl.BlockSpec(w5.shape, const, pipeline_mode=pl.Buffered(1)),
            pl.BlockSpec((1, CONV_HALO, CONV_DIM), lambda i: (i // nt, 0, 0)),
            pl.BlockSpec(w_dw.shape, const),
            pl.BlockSpec((1, CONV_DIM), const),
            pl.BlockSpec((1, CONV_DIM), const),
            pl.BlockSpec((1, CONV_DIM), const),
        ],
        out_specs=[pl.BlockSpec((tm, MEM_DIM), lambda i: (i, 0)),
                   pl.BlockSpec((tm, CONV_DIM), lambda i: (i, 0)),
                   pl.BlockSpec((1, CONV_HALO, CONV_DIM), lambda i: (i // nt, 0, 0))],
        out_shape=[jax.ShapeDtypeStruct((n, MEM_DIM), BF16),
                   jax.ShapeDtypeStruct((n, CONV_DIM), BF16),
                   jax.ShapeDtypeStruct((n // t_len, CONV_HALO, CONV_DIM), F32)],
        scratch_shapes=[pltpu.VMEM((CONV_HALO, CONV_DIM), F32),
                        pltpu.VMEM((CONV_WIDTH + 1, SUBLANES, CONV_DIM), F32)] + chain_scratch,
        compiler_params=_cparams(("arbitrary",)),
        name="glu_conv",
    )(xn, w5, hist, w_dw, b_dw, ln_g, ln_b)


def _mem_attn_kernel(q_ref, k_ref, v_ref, o_ref):
    for h in range(MEM_HEADS):
        sl = slice(h * MEM_HEAD_DIM, (h + 1) * MEM_HEAD_DIM)
        sc = _dot_nt(q_ref[0, :, sl], k_ref[0, :, sl]) * MEM_SCALE
        m = jnp.max(sc, axis=-1, keepdims=True)
        e = jnp.exp(sc - m)
        l = jnp.sum(e, axis=-1, keepdims=True)
        o = _dot(e.astype(BF16), v_ref[0, :, sl])
        o_ref[0, :, sl] = (o / l).astype(o_ref.dtype)


def _mem_attention(qm, mk, mv, tq):
    b, t, _ = qm.shape
    return pl.pallas_call(
        _mem_attn_kernel,
        grid=(b, t // tq),
        in_specs=[
            pl.BlockSpec((1, tq, MEM_DIM), lambda bi, i: (bi, i, 0)),
            pl.BlockSpec((1, MEM_LEN, MEM_DIM), lambda bi, i: (bi, 0, 0)),
            pl.BlockSpec((1, MEM_LEN, MEM_DIM), lambda bi, i: (bi, 0, 0)),
        ],
        out_specs=pl.BlockSpec((1, tq, MEM_DIM), lambda bi, i: (bi, i, 0)),
        out_shape=jax.ShapeDtypeStruct((b, t, MEM_DIM), BF16),
        compiler_params=_cparams(("parallel", "parallel")),
        name="mem_attention",
    )(qm, mk, mv)


def _norm_matmul_kernel(x_ref, g_ref, w_ref, o_ref):
    xn = _rms(x_ref[...], g_ref[...]).astype(BF16)
    o_ref[...] = _dot(xn, w_ref[...])


def _norm_matmul(x, g, w, tm, tn, nout=None):
    n, kdim = x.shape
    nout = w.shape[1] if nout is None else nout
    return pl.pallas_call(
        _norm_matmul_kernel,
        grid=(n // tm, nout // tn),
        in_specs=[
            pl.BlockSpec((tm, kdim), lambda i, j: (i, 0)),
            pl.BlockSpec((1, kdim), lambda i, j: (0, 0)),
            pl.BlockSpec((kdim, tn), lambda i, j: (0, j)),
        ],
        out_specs=pl.BlockSpec((tm, tn), lambda i, j: (i, j)),
        out_shape=jax.ShapeDtypeStruct((n, nout), F32),
        compiler_params=_cparams(("parallel", "arbitrary")),
        name="norm_matmul",
    )(x, g, w)


MIX_CHAINS = 4


def _mix_kernel(attn_ref, cv_ref, om_ref, xn_ref, wa_ref, wb_ref, wc_ref,
                wg0_ref, wg1_ref, wg2_ref, bg0_ref, bg1_ref, bg2_ref, mix_ref):
    tm = mix_ref.shape[0]
    rows = tm // MIX_CHAINS
    for ch in range(MIX_CHAINS):
        sl = slice(ch * rows, (ch + 1) * rows)
        xn = xn_ref[sl, :]
        a = _dot(attn_ref[sl, :], wa_ref[...])
        bo = _dot(cv_ref[sl, :], wb_ref[...])
        c = _dot(om_ref[sl, :], wc_ref[...])
        g0 = jax.nn.sigmoid(_dot(xn, wg0_ref[...]) + bg0_ref[...])
        g1 = jax.nn.sigmoid(_dot(xn, wg1_ref[...]) + bg1_ref[...])
        g2 = jax.nn.sigmoid(_dot(xn, wg2_ref[...]) + bg2_ref[...])
        mix_ref[sl, :] = (g0 * a + g1 * bo + g2 * c).astype(BF16)


def _branch_mix(attn, cv, om, xn, w_mla_o, w_conv_o, w_mem_o, wg, b_gate, tm, tn):
    n = xn.shape[0]
    nj = D_MODEL // tn
    row = lambda j, i: (i, 0)
    col = lambda j, i: (0, j)
    return pl.pallas_call(
        _mix_kernel,
        grid=(nj, n // tm),
        in_specs=[
            pl.BlockSpec((tm, N_HEADS * V_HEAD), row),
            pl.BlockSpec((tm, CONV_DIM), row),
            pl.BlockSpec((tm, MEM_DIM), row),
            pl.BlockSpec((tm, D_MODEL), row),
            pl.BlockSpec((N_HEADS * V_HEAD, tn), col),
            pl.BlockSpec((CONV_DIM, tn), col),
            pl.BlockSpec((MEM_DIM, tn), col),
            pl.BlockSpec((D_MODEL, tn), lambda j, i: (0, j)),
            pl.BlockSpec((D_MODEL, tn), lambda j, i: (0, nj + j)),
            pl.BlockSpec((D_MODEL, tn), lambda j, i: (0, 2 * nj + j)),
            pl.BlockSpec((1, tn), lambda j, i: (0, j)),
            pl.BlockSpec((1, tn), lambda j, i: (0, nj + j)),
            pl.BlockSpec((1, tn), lambda j, i: (0, 2 * nj + j)),
        ],
        out_specs=pl.BlockSpec((tm, tn), lambda j, i: (i, j)),
        out_shape=jax.ShapeDtypeStruct((n, D_MODEL), BF16),
        compiler_params=_cparams(("parallel", "parallel")),
        name="branch_mix",
    )(attn, cv, om, xn, w_mla_o, w_conv_o, w_mem_o, wg, wg, wg, b_gate, b_gate, b_gate)


def _out_proj_kernel(mix_ref, wo_ref, x_ref, gffn_ref, h_ref, hn_ref):
    h = x_ref[...] + _dot(mix_ref[...], wo_ref[...])
    h_ref[...] = h
    hn_ref[...] = _rms(h, gffn_ref[...]).astype(BF16)


def _out_proj(mix, w_out, x, g_ffn, tm):
    n = x.shape[0]
    row = lambda i: (i, 0)
    return pl.pallas_call(
        _out_proj_kernel,
        grid=(n // tm,),
        in_specs=[
            pl.BlockSpec((tm, D_MODEL), row),
            pl.BlockSpec((D_MODEL, D_MODEL), lambda i: (0, 0), pipeline_mode=pl.Buffered(1)),
            pl.BlockSpec((tm, D_MODEL), row),
            pl.BlockSpec((1, D_MODEL), lambda i: (0, 0)),
        ],
        out_specs=[pl.BlockSpec((tm, D_MODEL), row), pl.BlockSpec((tm, D_MODEL), row)],
        out_shape=[jax.ShapeDtypeStruct((n, D_MODEL), F32),
                   jax.ShapeDtypeStruct((n, D_MODEL), BF16)],
        compiler_params=_cparams(("parallel",)),
        name="out_proj",
    )(mix, w_out, x, g_ffn)


FFN_HALO = 16
FFN_CHAINS = 2


def _ffn_kernel(hn_ref, halo_ref, hist_ref, wa_ref, wv_ref, wdw_ref, bdw_ref, wd_ref, h_ref,
                gfin_ref, y_ref, abuf, *, tm, t_len):
    i = pl.program_id(0)
    j = pl.program_id(1)

    @pl.when(j == 0)
    def _():
        y_ref[...] = h_ref[...]

    def conv3(a, b0, rows):
        return (wdw_ref[2:3, :] * a
                + wdw_ref[1:2, :] * abuf[b0 - 1:b0 - 1 + rows, :]
                + wdw_ref[0:1, :] * abuf[b0 - 2:b0 - 2 + rows, :]
                + bdw_ref[...])

    nseq = hist_ref.shape[0]
    if nseq == 1:
        a_prev = _dot(halo_ref[...], wa_ref[...])
        at_seq_start = (i * tm) % t_len == 0
        abuf[0:FFN_HALO, :] = jnp.where(at_seq_start, hist_ref[0], a_prev)
        chains = FFN_CHAINS if tm % (FFN_CHAINS * FFN_HALO) == 0 else 1
        rows = tm // chains
        for ch in range(chains):
            r0 = ch * rows
            hn = hn_ref[r0:r0 + rows, :]
            a = _dot(hn, wa_ref[...])
            val = _dot(hn, wv_ref[...])
            abuf[FFN_HALO + r0:FFN_HALO + r0 + rows, :] = a
            conv = conv3(a, FFN_HALO + r0, rows)
            act = (conv * jax.nn.sigmoid(conv) * val).astype(BF16)
            y_ref[r0:r0 + rows, :] += _dot(act, wd_ref[...])
    else:
        seg = tm // nseq
        hn = hn_ref[...]
        a = _dot(hn, wa_ref[...])
        val = _dot(hn, wv_ref[...])
        convs = []
        for s in range(nseq):
            base = s * (seg + FFN_HALO)
            a_s = a[s * seg:(s + 1) * seg, :]
            abuf[base:base + FFN_HALO, :] = hist_ref[s]
            abuf[base + FFN_HALO:base + FFN_HALO + seg, :] = a_s
            convs.append(conv3(a_s, base + FFN_HALO, seg))
        conv = jnp.concatenate(convs, axis=0)
        act = (conv * jax.nn.sigmoid(conv) * val).astype(BF16)
        y_ref[...] += _dot(act, wd_ref[...])

    @pl.when(j == pl.num_programs(1) - 1)
    def _():
        y_ref[...] = _rms(y_ref[...], gfin_ref[...])


def _ffn(hn, hist, w_up, w_dw, b_dw, w_down, h, g_final, t_len, tm, tn):
    n = hn.shape[0]
    nj = D_FF // tn
    r = tm // FFN_HALO
    nseq = max(1, tm // t_len)
    assert tm % t_len == 0 or t_len % tm == 0
    kernel = functools.partial(_ffn_kernel, tm=tm, t_len=t_len)
    return pl.pallas_call(
        kernel,
        grid=(n // tm, nj),
        in_specs=[
            pl.BlockSpec((tm, D_MODEL), lambda i, j: (i, 0)),
            pl.BlockSpec((FFN_HALO, D_MODEL), lambda i, j: (jnp.maximum(i * r - 1, 0), 0)),
            pl.BlockSpec((nseq, FFN_HALO, tn), lambda i, j: ((i * tm) // (t_len * nseq), 0, j)),
            pl.BlockSpec((D_MODEL, tn), lambda i, j: (0, j)),
            pl.BlockSpec((D_MODEL, tn), lambda i, j: (0, nj + j)),
            pl.BlockSpec((FFN_CONV_WIDTH, tn), lambda i, j: (0, j)),
            pl.BlockSpec((1, tn), lambda i, j: (0, j)),
            pl.BlockSpec((tn, D_MODEL), lambda i, j: (j, 0)),
            pl.BlockSpec((tm, D_MODEL), lambda i, j: (i, 0), pipeline_mode=pl.Buffered(1)),
            pl.BlockSpec((1, D_MODEL), lambda i, j: (0, 0)),
        ],
        out_specs=pl.BlockSpec((tm, D_MODEL), lambda i, j: (i, 0)),
        out_shape=jax.ShapeDtypeStruct((n, D_MODEL), F32),
        scratch_shapes=[pltpu.VMEM((tm + nseq * FFN_HALO, tn), F32)],
        compiler_params=_cparams(("parallel", "arbitrary")),
        name="conv_ffn",
    )(hn, hn, hist, w_up, w_up, w_dw, b_dw, w_down, h, g_final)


def _rope_tables(pos):
    half = QK_ROPE // 2
    inv_freq = ROPE_THETA ** (-jnp.arange(half, dtype=F32) / half)
    ang = pos.astype(F32)[:, None] * inv_freq[None, :]
    cos = jnp.cos(ang)
    sin = jnp.sin(ang)
    cos_t = jnp.tile(cos, (1, LANES // half))
    sin_t = jnp.tile(jnp.concatenate([-sin, sin], axis=1), (1, LANES // QK_ROPE))
    return cos_t, sin_t


def _swap_halves(w):
    half = QK_ROPE // 2
    return jnp.concatenate([w[..., half:], w[..., :half]], axis=-1)


def _prep_weights(w_in, w_uq, w_ukv, w_mla_o, w_conv_o, w_mem_o, w_out, w_up, w_down,
                  w_mem_k, w_mem_v, w_conv_dw):
    kr = w_in[:, OFF_KR:OFF_GLU]
    kr_sw = _swap_halves(kr)
    w1 = jnp.concatenate([w_in[:, :OFF_KR], kr, kr, kr_sw, kr_sw], axis=1).astype(BF16)
    w5 = w_in[:, OFF_GLU:OFF_GATE].astype(BF16)
    wg = w_in[:, OFF_GATE:].astype(BF16)
    nope = w_uq[:, :, :QK_NOPE].reshape(Q_LORA, N_HEADS // 2, 2 * QK_NOPE)
    rope = w_uq[:, :, QK_NOPE:]
    rope_p = rope.reshape(Q_LORA, N_HEADS // 2, 2 * QK_ROPE)
    rope_sw = _swap_halves(rope).reshape(Q_LORA, N_HEADS // 2, 2 * QK_ROPE)
    w2 = jnp.concatenate([nope, rope_p, rope_sw], axis=2).reshape(Q_LORA, -1).astype(BF16)
    w3 = jnp.concatenate([w_ukv[:, :, :QK_NOPE].reshape(KV_LORA, -1),
                          w_ukv[:, :, QK_NOPE:].reshape(KV_LORA, -1)], axis=1).astype(BF16)
    wmem = jnp.concatenate([w_mem_k, w_mem_v], axis=1).astype(BF16)
    wdw = jnp.concatenate([w_conv_dw, jnp.zeros((1, CONV_DIM), F32)], axis=0)
    hk = N_HEADS * QK_NOPE
    return dict(w1=w1, w5=w5, wg=wg, w2=w2, w3=w3, w3k=w3[:, :hk], w3vt=w3[:, hk:].T, wmem=wmem, wdw=wdw,
                w_mla_o=w_mla_o.astype(BF16), w_conv_o=w_conv_o.astype(BF16),
                w_mem_o=w_mem_o.astype(BF16), w_out=w_out.astype(BF16),
                w_up=w_up.astype(BF16), w_down=w_down.astype(BF16))


def _pad_hist(hist, rows):
    b, k, c = hist.shape
    return jnp.concatenate([jnp.zeros((b, rows - k, c), hist.dtype), hist], axis=1)


def _encoder_layer(x, past_len, hist_conv, hist_ffn, past_ckv, past_krope, mem_k, mem_v, wts, prm,
                   cfg):
    b, t, _ = x.shape
    n = b * t
    pos = past_len + jnp.arange(t)
    cos_t, sin_t = _rope_tables(pos)
    tm = cfg["tm"]
    if tm > t:
        cos_t = jnp.tile(cos_t, (tm // t, 1))
        sin_t = jnp.tile(sin_t, (tm // t, 1))
    x2 = x.reshape(n, D_MODEL)

    xn, cqn, ckv, ckv_b, krope, krpad = _proj_lat(
        x2, prm["g_mix"], wts["w1"], prm["g_cq"], prm["g_ckv"], cos_t, sin_t, tm)
    if past_len:
        pk = past_krope.astype(BF16)
        zeros = jnp.zeros_like(pk)
        past_pad = jnp.concatenate([pk, zeros, zeros, pk], axis=-1)
        ckv_all = jnp.concatenate([past_ckv.astype(BF16), ckv_b.reshape(b, t, KV_LORA)], axis=1)
        kr_all = jnp.concatenate([past_pad, krpad.reshape(b, t, Q_HEAD_W)], axis=1)
    else:
        ckv_all = ckv_b.reshape(b, t, KV_LORA)
        kr_all = krpad.reshape(b, t, Q_HEAD_W)
    s = past_len + t
    q = _proj_q(cqn, wts["w2"], cos_t, sin_t, tm)
    if cfg["values_transposed"]:
        kf, vt = _proj_kvt(ckv_all.reshape(b * s, KV_LORA), kr_all.reshape(b * s, Q_HEAD_W),
                           wts["w3k"], wts["w3vt"], b, s, _row_tile(s, cfg["tm_kv"], LANES))
        attn = _attention_vt(q.reshape(b, t, -1), kf.reshape(b, s, -1), vt,
                             past_len, cfg["tq"], cfg["tk"], cfg["hg"])
    else:
        kf, vf = _proj_kv(ckv_all.reshape(b * s, KV_LORA), kr_all.reshape(b * s, Q_HEAD_W), wts["w3"],
                          _row_tile(b * s, cfg["tm_kv"]))
        attn = _attention(q.reshape(b, t, -1), kf.reshape(b, s, -1), vf.reshape(b, s, -1),
                          past_len, cfg["tq"], cfg["tk"], cfg["hg"])

    qm, cv, glu_tail = _glu_conv(xn, wts["w5"], _pad_hist(hist_conv, CONV_HALO), wts["wdw"],
                                 prm["b_conv_dw"], prm["ln_conv_g"], prm["ln_conv_b"], t, cfg["tm_glu"])
    om = _mem_attention(qm.reshape(b, t, MEM_DIM), mem_k, mem_v, cfg["tq_mem"])

    mix = _branch_mix(attn.reshape(n, -1), cv.reshape(n, CONV_DIM), om.reshape(n, MEM_DIM), xn,
                      wts["w_mla_o"], wts["w_conv_o"], wts["w_mem_o"], wts["wg"], prm["b_gate"],
                      cfg["tm_mix"], cfg["tn_mix"])
    h, hn = _out_proj(mix, wts["w_out"], x2, prm["g_ffn"], tm)
    y = _ffn(hn, _pad_hist(hist_ffn, FFN_HALO), wts["w_up"], prm["w_ffn_dw"], prm["b_ffn_dw"],
             wts["w_down"], h, prm["g_final"], t, cfg["tm_ffn"], cfg["tn_ffn"])

    new_hist_conv = glu_tail[:, CONV_HALO - (CONV_WIDTH - 1):]
    h_tail = h.reshape(b, t, D_MODEL)[:, t - 8:]
    return (y.reshape(b, t, D_MODEL), ckv.reshape(b, t, KV_LORA), krope.reshape(b, t, QK_ROPE),
            new_hist_conv, h_tail)


def kernel(x_prompt, x_sample, mem_prompt, cache_ckv, cache_krope, state_conv, state_ffn_conv,
           cache_mem_k, cache_mem_v, g_mix, w_in, g_cq, w_uq, g_ckv, w_ukv, w_mla_o,
           w_conv_dw, b_conv_dw, ln_conv_g, ln_conv_b, w_conv_o, g_mem, w_mem_k, w_mem_v,
           w_mem_o, b_gate, w_out, g_ffn, w_up, w_ffn_dw, b_ffn_dw, w_down, g_final):
    depth = g_mix.shape[0]
    assert depth == 1
    l = 0
    bp, tp, _ = x_prompt.shape
    bs, ts, _ = x_sample.shape
    past = cache_ckv.shape[2]
    assert tp >= CONV_HALO and ts >= CONV_HALO

    wts = _prep_weights(w_in[l], w_uq[l], w_ukv[l], w_mla_o[l], w_conv_o[l], w_mem_o[l], w_out[l],
                        w_up[l], w_down[l], w_mem_k[l], w_mem_v[l], w_conv_dw[l])
    row = lambda a: a.reshape(1, -1)
    prm = dict(g_mix=row(g_mix[l]), g_cq=row(g_cq[l]), g_ckv=row(g_ckv[l]),
               b_conv_dw=row(b_conv_dw[l]), ln_conv_g=row(ln_conv_g[l]), ln_conv_b=row(ln_conv_b[l]),
               b_gate=row(b_gate[l]), g_ffn=row(g_ffn[l]), w_ffn_dw=w_ffn_dw[l],
               b_ffn_dw=row(b_ffn_dw[l]), g_final=row(g_final))

    mem2 = mem_prompt.reshape(bp * MEM_LEN, D_MODEL)
    mkv = _norm_matmul(mem2, row(g_mem[l]), wts["wmem"], 256, 512)
    mk_p = mkv[:, :MEM_DIM].reshape(bp, MEM_LEN, MEM_DIM)
    mv_p = mkv[:, MEM_DIM:].reshape(bp, MEM_LEN, MEM_DIM)

    cfg_p = dict(values_transposed=True, tm=512, tm_kv=512, tq=512, tk=512, hg=16, tm_glu=512, tq_mem=512,
                 tm_mix=1024, tn_mix=256, tm_ffn=1024, tn_ffn=512)
    n_s = bs * ts
    cfg_s = dict(values_transposed=False, tm=n_s, tm_kv=512, tq=ts, tk=past + ts, hg=8, tm_glu=ts, tq_mem=ts,
                 tm_mix=n_s, tn_mix=256, tm_ffn=n_s, tn_ffn=512)

    zeros_conv = jnp.zeros((bp, CONV_WIDTH - 1, CONV_DIM), F32)
    zeros_ffn = jnp.zeros((bp, FFN_CONV_WIDTH - 1, D_FF), F32)
    yp, ckv_p, kr_p, cs_p, htail_p = _encoder_layer(
        x_prompt, 0, zeros_conv, zeros_ffn, None, None,
        mk_p.astype(BF16), mv_p.astype(BF16), wts, prm, cfg_p)
    ys, ckv_s, kr_s, cs_s, htail_s = _encoder_layer(
        x_sample, past, state_conv[l], state_ffn_conv[l], cache_ckv[l], cache_krope[l],
        cache_mem_k[l].reshape(bs, MEM_LEN, MEM_DIM).astype(BF16),
        cache_mem_v[l].reshape(bs, MEM_LEN, MEM_DIM).astype(BF16), wts, prm, cfg_s)

    tails = jnp.concatenate([htail_p.reshape(bp * 8, D_MODEL), htail_s.reshape(bs * 8, D_MODEL)], axis=0)
    a_tail = _norm_matmul(tails, prm["g_ffn"], wts["w_up"], tails.shape[0], 512, nout=D_FF)
    fs_p = a_tail[:bp * 8].reshape(bp, 8, D_FF)[:, 8 - (FFN_CONV_WIDTH - 1):]
    fs_s = a_tail[bp * 8:].reshape(bs, 8, D_FF)[:, 8 - (FFN_CONV_WIDTH - 1):]

    st = lambda a: a[None]
    return (yp, ys, st(ckv_p), st(kr_p), st(cs_p), st(fs_p),
            st(mk_p.reshape(bp, MEM_LEN, MEM_HEADS, MEM_HEAD_DIM)),
            st(mv_p.reshape(bp, MEM_LEN, MEM_HEADS, MEM_HEAD_DIM)),
            st(ckv_s), st(kr_s), st(cs_s), st(fs_s))
```

```python
import functools

import numpy as np
import jax
import jax.numpy as jnp
from jax import lax
from jax.experimental import pallas as pl
from jax.experimental.pallas import tpu as pltpu

D_MODEL = 2048
CHUNK = 64
EPS = 1e-6
N_HEADS = 16
QK_NOPE = 128
QK_ROPE = 64
V_HEAD = 128
Q_LORA = 512
KV_LORA = 512
ROPE_THETA = 10000.0
MLA_SCALE = (QK_NOPE + QK_ROPE) ** -0.5
CONV_DIM = 1024
CONV_WIDTH = 31
MEM_LEN = 256
MEM_HEADS = 4
MEM_HEAD_DIM = 256
MEM_DIM = MEM_HEADS * MEM_HEAD_DIM
MEM_SCALE = MEM_HEAD_DIM ** -0.5
N_BRANCH = 3
D_FF = 5632
FFN_CONV_WIDTH = 3
OFF_CKV = Q_LORA
OFF_KR = OFF_CKV + KV_LORA
OFF_GLU = OFF_KR + QK_ROPE
OFF_QM = OFF_GLU + 2 * CONV_DIM
OFF_GATE = OFF_QM + MEM_DIM

LANES = 128
V7X_VMEM_LIMIT = 56 * 1024 * 1024
Q_HEAD_W = 2 * LANES
V_HEAD_W = 2 * LANES
NEG_BIG = -1e30
Q_SCALE = MLA_SCALE * float(np.log2(np.e))

BF16 = jnp.bfloat16
F32 = jnp.float32


def _cparams(sem, vmem=V7X_VMEM_LIMIT):
    return pltpu.CompilerParams(dimension_semantics=sem, vmem_limit_bytes=vmem)


def _row_tile(n, cap, mult=16):
    t = min(cap, n)
    t -= t % mult
    while t > mult and n % t:
        t -= mult
    assert t >= mult and n % t == 0, (n, cap)
    return t


def _rms(x, g):
    y = x * lax.rsqrt(jnp.mean(x * x, axis=-1, keepdims=True) + EPS)
    return y * g


def _dot(a, b):
    return jnp.dot(a, b, preferred_element_type=F32)


def _dot_nt(a, b):
    return lax.dot_general(a, b, (((1,), (1,)), ((), ())), preferred_element_type=F32)


def _proj_lat_kernel(x_ref, gmix_ref, w_ref, gcq_ref, gckv_ref, cos_ref, sin_ref,
                     xn_ref, cqn_ref, ckv_ref, ckvb_ref, kr_ref, krpad_ref):
    xn = _rms(x_ref[...], gmix_ref[...]).astype(BF16)
    xn_ref[...] = xn
    z = _dot(xn, w_ref[...])
    cqn_ref[...] = _rms(z[:, :Q_LORA], gcq_ref[...]).astype(BF16)
    ckv = _rms(z[:, OFF_CKV:OFF_KR], gckv_ref[...])
    ckv_ref[...] = ckv
    ckvb_ref[...] = ckv.astype(BF16)
    r = z[:, OFF_KR:OFF_KR + LANES] * cos_ref[...] + z[:, OFF_KR + LANES:] * sin_ref[...]
    kr_ref[...] = r[:, :QK_ROPE]
    lane = lax.broadcasted_iota(jnp.int32, r.shape, 1)
    zero = jnp.zeros_like(r)
    krpad_ref[:, :LANES] = jnp.where(lane < QK_ROPE, r, zero).astype(BF16)
    krpad_ref[:, LANES:] = jnp.where(lane >= QK_ROPE, r, zero).astype(BF16)


def _proj_lat(x, g_mix, w1, g_cq, g_ckv, cos_t, sin_t, tm):
    n = x.shape[0]
    nt = cos_t.shape[0] // tm
    row = lambda i: (i, 0)
    const = lambda i: (0, 0)
    tab = lambda i: (i % nt, 0)
    return pl.pallas_call(
        _proj_lat_kernel,
        grid=(n // tm,),
        in_specs=[
            pl.BlockSpec((tm, D_MODEL), row),
            pl.BlockSpec((1, D_MODEL), const),
            pl.BlockSpec(w1.shape, const),
            pl.BlockSpec((1, Q_LORA), const),
            pl.BlockSpec((1, KV_LORA), const),
            pl.BlockSpec((tm, LANES), tab),
            pl.BlockSpec((tm, LANES), tab),
        ],
        out_specs=[
            pl.BlockSpec((tm, D_MODEL), row),
            pl.BlockSpec((tm, Q_LORA), row),
            pl.BlockSpec((tm, KV_LORA), row),
            pl.BlockSpec((tm, KV_LORA), row),
            pl.BlockSpec((tm, QK_ROPE), row),
            pl.BlockSpec((tm, Q_HEAD_W), row),
        ],
        out_shape=[
            jax.ShapeDtypeStruct((n, D_MODEL), BF16),
            jax.ShapeDtypeStruct((n, Q_LORA), BF16),
            jax.ShapeDtypeStruct((n, KV_LORA), F32),
            jax.ShapeDtypeStruct((n, KV_LORA), BF16),
            jax.ShapeDtypeStruct((n, QK_ROPE), F32),
            jax.ShapeDtypeStruct((n, Q_HEAD_W), BF16),
        ],
        compiler_params=_cparams(("parallel",)),
        name="proj_lat",
    )(x, g_mix, w1, g_cq, g_ckv, cos_t, sin_t)


def _proj_q_kernel(cqn_ref, w_ref, cos_ref, sin_ref, q_ref):
    cqn = cqn_ref[...]
    cos = cos_ref[...]
    sin = sin_ref[...]
    pw = 4 * LANES
    for p in range(N_HEADS // 2):
        r = _dot(cqn, w_ref[:, p * pw:(p + 1) * pw])
        rp = ((r[:, 2 * LANES:3 * LANES] * cos + r[:, 3 * LANES:] * sin) * Q_SCALE).astype(BF16)
        q_ref[:, p * pw:p * pw + LANES] = (r[:, :LANES] * Q_SCALE).astype(BF16)
        q_ref[:, p * pw + LANES:p * pw + 2 * LANES] = rp
        q_ref[:, p * pw + 2 * LANES:p * pw + 3 * LANES] = (r[:, LANES:2 * LANES] * Q_SCALE).astype(BF16)
        q_ref[:, p * pw + 3 * LANES:(p + 1) * pw] = rp


def _proj_q(cqn, w2, cos_t, sin_t, tm):
    n = cqn.shape[0]
    nt = cos_t.shape[0] // tm
    return pl.pallas_call(
        _proj_q_kernel,
        grid=(n // tm,),
        in_specs=[
            pl.BlockSpec((tm, Q_LORA), lambda i: (i, 0)),
            pl.BlockSpec(w2.shape, lambda i: (0, 0)),
            pl.BlockSpec((tm, LANES), lambda i: (i % nt, 0)),
            pl.BlockSpec((tm, LANES), lambda i: (i % nt, 0)),
        ],
        out_specs=pl.BlockSpec((tm, N_HEADS * Q_HEAD_W), lambda i: (i, 0)),
        out_shape=jax.ShapeDtypeStruct((n, N_HEADS * Q_HEAD_W), BF16),
        compiler_params=_cparams(("parallel",)),
        name="proj_q",
    )(cqn, w2, cos_t, sin_t)


def _proj_kv_kernel(c_ref, kr_ref, w_ref, k_ref, v_ref):
    c = c_ref[...]
    hw = N_HEADS * QK_NOPE
    kn = _dot(c, w_ref[:, :hw]).astype(BF16)
    vv = _dot(c, w_ref[:, hw:]).astype(BF16)
    ones = jnp.ones((c.shape[0], LANES), BF16)
    for h in range(N_HEADS):
        par = h % 2
        k_ref[:, h * Q_HEAD_W:h * Q_HEAD_W + QK_NOPE] = kn[:, h * QK_NOPE:(h + 1) * QK_NOPE]
        k_ref[:, h * Q_HEAD_W + QK_NOPE:(h + 1) * Q_HEAD_W] = kr_ref[:, par * LANES:(par + 1) * LANES]
        v_ref[:, h * V_HEAD_W:h * V_HEAD_W + V_HEAD] = vv[:, h * V_HEAD:(h + 1) * V_HEAD]
        v_ref[:, h * V_HEAD_W + V_HEAD:(h + 1) * V_HEAD_W] = ones


def _proj_kv(ckv_b, krpad, w3, tm):
    n = ckv_b.shape[0]
    return pl.pallas_call(
        _proj_kv_kernel,
        grid=(n // tm,),
        in_specs=[
            pl.BlockSpec((tm, KV_LORA), lambda i: (i, 0)),
            pl.BlockSpec((tm, Q_HEAD_W), lambda i: (i, 0)),
            pl.BlockSpec(w3.shape, lambda i: (0, 0)),
        ],
        out_specs=[pl.BlockSpec((tm, N_HEADS * Q_HEAD_W), lambda i: (i, 0)),
                   pl.BlockSpec((tm, N_HEADS * V_HEAD_W), lambda i: (i, 0))],
        out_shape=[jax.ShapeDtypeStruct((n, N_HEADS * Q_HEAD_W), BF16),
                   jax.ShapeDtypeStruct((n, N_HEADS * V_HEAD_W), BF16)],
        compiler_params=_cparams(("parallel",)),
        name="proj_kv",
    )(ckv_b, krpad, w3)


def _attn_schedule(t, s, p, tq, tk):
    qi_l, ki_l, fl_l = [], [], []
    for qi in range(t // tq):
        q_lo = p + qi * tq
        q_hi = p + (qi + 1) * tq - 1
        kmax = min(s, (q_hi // CHUNK + 1) * CHUNK)
        full_vis = (q_lo // CHUNK + 1) * CHUNK
        nk = -(-kmax // tk)
        for ki in range(nk):
            need_mask = (ki + 1) * tk > full_vis
            qi_l.append(qi)
            ki_l.append(ki)
            fl_l.append(int(ki == 0) | (int(ki == nk - 1) << 1) | (int(need_mask) << 2))
    return (np.asarray(qi_l, np.int32), np.asarray(ki_l, np.int32), np.asarray(fl_l, np.int32))


def _attn_kernel(qi_tab, ki_tab, fl_tab, q_ref, k_ref, v_ref, o_ref, m_scr, acc_scr,
                 *, hg, tq, tk, past):
    step = pl.program_id(2)
    flags = fl_tab[step]
    qi = qi_tab[step]
    ki = ki_tab[step]

    @pl.when((flags & 1) != 0)
    def _():
        m_scr[...] = jnp.full(m_scr.shape, NEG_BIG, F32)
        acc_scr[...] = jnp.zeros(acc_scr.shape, F32)

    def body(masked):
        if masked:
            qpos = past + qi * tq + lax.broadcasted_iota(jnp.int32, (tq, tk), 0)
            kpos = ki * tk + lax.broadcasted_iota(jnp.int32, (tq, tk), 1)
            visible = (kpos // CHUNK) <= (qpos // CHUNK)
        for h in range(hg):
            q = q_ref[0, :, h * Q_HEAD_W:(h + 1) * Q_HEAD_W]
            k = k_ref[0, :, h * Q_HEAD_W:(h + 1) * Q_HEAD_W]
            sc = _dot_nt(q, k)
            if masked:
                sc = jnp.where(visible, sc, NEG_BIG)
            m_prev = m_scr[h]
            m_cur = jnp.max(sc, axis=1, keepdims=True)
            m_next = jnp.maximum(m_prev, m_cur)
            if tk % LANES == 0:
                p = jnp.exp2(sc - jnp.concatenate([m_next] * (tk // LANES), axis=1))
            else:
                p = jnp.exp2(sc - m_next[:, :1])
            alpha = jnp.exp2(m_prev - m_next)
            m_scr[h] = m_next
            pv = _dot(p.astype(BF16), v_ref[0, :, h * V_HEAD_W:(h + 1) * V_HEAD_W])
            acc_scr[h] = acc_scr[h] * jnp.concatenate([alpha] * (V_HEAD_W // LANES), axis=1) + pv

    @pl.when((flags & 4) != 0)
    def _():
        body(True)

    @pl.when((flags & 4) == 0)
    def _():
        body(False)

    @pl.when((flags & 2) != 0)
    def _():
        for h in range(hg):
            acc = acc_scr[h]
            o_ref[0, :, h * V_HEAD:(h + 1) * V_HEAD] = (acc[:, :V_HEAD] / acc[:, V_HEAD:]).astype(o_ref.dtype)


def _attention(q, k, v, past, tq, tk, hg):
    b, t, _ = q.shape
    s = k.shape[1]
    qi_np, ki_np, fl_np = _attn_schedule(t, s, past, tq, tk)
    n_steps = int(qi_np.shape[0])
    kernel = functools.partial(_attn_kernel, hg=hg, tq=tq, tk=tk, past=past)
    grid_spec = pltpu.PrefetchScalarGridSpec(
        num_scalar_prefetch=3,
        grid=(b, N_HEADS // hg, n_steps),
        in_specs=[
            pl.BlockSpec((1, tq, hg * Q_HEAD_W), lambda bi, g, st, qt, kt, ft: (bi, qt[st], g)),
            pl.BlockSpec((1, tk, hg * Q_HEAD_W), lambda bi, g, st, qt, kt, ft: (bi, kt[st], g)),
            pl.BlockSpec((1, tk, hg * V_HEAD_W), lambda bi, g, st, qt, kt, ft: (bi, kt[st], g)),
        ],
        out_specs=pl.BlockSpec((1, tq, hg * V_HEAD), lambda bi, g, st, qt, kt, ft: (bi, qt[st], g)),
        scratch_shapes=[
            pltpu.VMEM((hg, tq, LANES), F32),
            pltpu.VMEM((hg, tq, V_HEAD_W), F32),
        ],
    )
    return pl.pallas_call(
        kernel,
        grid_spec=grid_spec,
        out_shape=jax.ShapeDtypeStruct((b, t, N_HEADS * V_HEAD), BF16),
        compiler_params=_cparams(("parallel", "parallel", "arbitrary")),
        name="mla_attention",
    )(jnp.asarray(qi_np), jnp.asarray(ki_np), jnp.asarray(fl_np), q, k, v)


CONV_HALO = 32
SUBLANES = 8
CONV_ROWS = 32
GLU_CHAINS = 2


def _glu_conv_kernel(xn_ref, w_ref, hist_ref, wdw_ref, b_ref, g_ref, beta_ref, mk_ref, mv_ref,
                     om_ref, cv_ref, tail_ref, carry, wb, *chain_scratch, tm, t_len):
    i = pl.program_id(0)
    gbufs, shifteds, cbufs = chain_scratch[0::3], chain_scratch[1::3], chain_scratch[2::3]
    chains = len(gbufs)
    rows = tm // chains

    @pl.when((i * tm) % t_len == 0)
    def _():
        carry[...] = hist_ref[0]

    for k in range(CONV_WIDTH):
        wb[k] = jnp.broadcast_to(wdw_ref[k:k + 1, :], (SUBLANES, CONV_DIM))
    wb[CONV_WIDTH] = jnp.broadcast_to(b_ref[...], (SUBLANES, CONV_DIM))

    off = CONV_HALO - (CONV_WIDTH - 1)
    groups = CONV_ROWS // SUBLANES
    for ch in range(chains):
        r0 = ch * rows
        gbuf, shifted, cbuf = gbufs[ch], shifteds[ch], cbufs[ch]
        gbuf[0:CONV_HALO, :] = carry[...] if ch == 0 else gbufs[ch - 1][rows:rows + CONV_HALO, :]
        xn = xn_ref[r0:r0 + rows, :]
        a = _dot(xn, w_ref[:, :CONV_DIM])
        bgate = _dot(xn, w_ref[:, CONV_DIM:2 * CONV_DIM])
        gbuf[CONV_HALO:, :] = a * jax.nn.sigmoid(bgate)
        qm = _dot(xn, w_ref[:, 2 * CONV_DIM:]).astype(BF16)
        for h in range(MEM_HEADS):
            sl = slice(h * MEM_HEAD_DIM, (h + 1) * MEM_HEAD_DIM)
            sc = _dot_nt(qm[:, sl], mk_ref[0, :, sl]) * MEM_SCALE
            e = jnp.exp(sc - jnp.max(sc, axis=-1, keepdims=True))
            o = _dot(e.astype(BF16), mv_ref[0, :, sl])
            om_ref[r0:r0 + rows, sl] = (o / jnp.sum(e, axis=-1, keepdims=True)).astype(om_ref.dtype)
        span = rows + CONV_HALO - SUBLANES
        for p in range(1, SUBLANES):
            shifted[p - 1] = gbuf[p:p + span, :]
        for c0 in range(0, rows, CONV_ROWS):
            accs = [wb[CONV_WIDTH]] * groups
            for k in range(CONV_WIDTH):
                s = off + k + c0
                p = s % SUBLANES
                wk = wb[k]
                for g in range(groups):
                    if p == 0:
                        rws = gbuf[s + g * SUBLANES:s + (g + 1) * SUBLANES, :]
                    else:
                        rws = shifted[p - 1, s - p + g * SUBLANES:s - p + (g + 1) * SUBLANES, :]
                    accs[g] = accs[g] + wk * rws
            for g in range(groups):
                cbuf[c0 + g * SUBLANES:c0 + (g + 1) * SUBLANES, :] = accs[g]
        cv = cbuf[...]
        mu = jnp.mean(cv, axis=-1, keepdims=True)
        d = cv - mu
        var = jnp.mean(d * d, axis=-1, keepdims=True)
        y = d * lax.rsqrt(var + EPS) * g_ref[...] + beta_ref[...]
        cv_ref[r0:r0 + rows, :] = (y * jax.nn.sigmoid(y)).astype(cv_ref.dtype)

    last = gbufs[-1][rows:rows + CONV_HALO, :]
    tail_ref[0] = last
    carry[...] = last


def _glu_conv(xn, w5, hist, w_dw, b_dw, ln_g, ln_b, mem_k, mem_v, t_len, tm):
    n = xn.shape[0]
    assert t_len % tm == 0 and tm % CONV_ROWS == 0 and tm >= CONV_HALO
    nt = t_len // tm
    chains = GLU_CHAINS if tm % (GLU_CHAINS * CONV_ROWS) == 0 else 1
    rows = tm // chains
    chain_scratch = [pltpu.VMEM((rows + CONV_HALO, CONV_DIM), F32),
                     pltpu.VMEM((SUBLANES - 1, rows + CONV_HALO - SUBLANES, CONV_DIM), F32),
                     pltpu.VMEM((rows, CONV_DIM), F32)] * chains
    kernel = functools.partial(_glu_conv_kernel, tm=tm, t_len=t_len)
    const = lambda i: (0, 0)
    return pl.pallas_call(
        kernel,
        grid=(n // tm,),
        in_specs=[
            pl.BlockSpec((tm, D_MODEL), lambda i: (i, 0)),
            pl.BlockSpec(w5.shape, const, pipeline_mode=pl.Buffered(1)),
            pl.BlockSpec((1, CONV_HALO, CONV_DIM), lambda i: (i // nt, 0, 0)),
            pl.BlockSpec(w_dw.shape, const),
            pl.BlockSpec((1, CONV_DIM), const),
            pl.BlockSpec((1, CONV_DIM), const),
            pl.BlockSpec((1, CONV_DIM), const),
            pl.BlockSpec((1, MEM_LEN, MEM_DIM), lambda i: (i // nt, 0, 0)),
            pl.BlockSpec((1, MEM_LEN, MEM_DIM), lambda i: (i // nt, 0, 0)),
        ],
        out_specs=[pl.BlockSpec((tm, MEM_DIM), lambda i: (i, 0)),
                   pl.BlockSpec((tm, CONV_DIM), lambda i: (i, 0)),
                   pl.BlockSpec((1, CONV_HALO, CONV_DIM), lambda i: (i // nt, 0, 0))],
        out_shape=[jax.ShapeDtypeStruct((n, MEM_DIM), BF16),
                   jax.ShapeDtypeStruct((n, CONV_DIM), BF16),
                   jax.ShapeDtypeStruct((n // t_len, CONV_HALO, CONV_DIM), F32)],
        scratch_shapes=[pltpu.VMEM((CONV_HALO, CONV_DIM), F32),
                        pltpu.VMEM((CONV_WIDTH + 1, SUBLANES, CONV_DIM), F32)] + chain_scratch,
        compiler_params=_cparams(("arbitrary",)),
        name="glu_conv",
    )(xn, w5, hist, w_dw, b_dw, ln_g, ln_b, mem_k, mem_v)


def _norm_matmul_kernel(x_ref, g_ref, w_ref, o_ref):
    xn = _rms(x_ref[...], g_ref[...]).astype(BF16)
    o_ref[...] = _dot(xn, w_ref[...])


def _norm_matmul(x, g, w, tm, tn, nout=None):
    n, kdim = x.shape
    nout = w.shape[1] if nout is None else nout
    return pl.pallas_call(
        _norm_matmul_kernel,
        grid=(n // tm, nout // tn),
        in_specs=[
            pl.BlockSpec((tm, kdim), lambda i, j: (i, 0)),
            pl.BlockSpec((1, kdim), lambda i, j: (0, 0)),
            pl.BlockSpec((kdim, tn), lambda i, j: (0, j)),
        ],
        out_specs=pl.BlockSpec((tm, tn), lambda i, j: (i, j)),
        out_shape=jax.ShapeDtypeStruct((n, nout), F32),
        compiler_params=_cparams(("parallel", "arbitrary")),
        name="norm_matmul",
    )(x, g, w)


MIX_CHAINS = 4


def _mix_kernel(attn_ref, cv_ref, om_ref, xn_ref, wa_ref, wb_ref, wc_ref,
                wg0_ref, wg1_ref, wg2_ref, bg0_ref, bg1_ref, bg2_ref, mix_ref):
    tm = mix_ref.shape[0]
    rows = tm // MIX_CHAINS
    for ch in range(MIX_CHAINS):
        sl = slice(ch * rows, (ch + 1) * rows)
        xn = xn_ref[sl, :]
        a = _dot(attn_ref[sl, :], wa_ref[...])
        bo = _dot(cv_ref[sl, :], wb_ref[...])
        c = _dot(om_ref[sl, :], wc_ref[...])
        g0 = jax.nn.sigmoid(_dot(xn, wg0_ref[...]) + bg0_ref[...])
        g1 = jax.nn.sigmoid(_dot(xn, wg1_ref[...]) + bg1_ref[...])
        g2 = jax.nn.sigmoid(_dot(xn, wg2_ref[...]) + bg2_ref[...])
        mix_ref[sl, :] = (g0 * a + g1 * bo + g2 * c).astype(BF16)


def _branch_mix(attn, cv, om, xn, w_mla_o, w_conv_o, w_mem_o, wg, b_gate, tm, tn):
    n = xn.shape[0]
    nj = D_MODEL // tn
    row = lambda j, i: (i, 0)
    col = lambda j, i: (0, j)
    return pl.pallas_call(
        _mix_kernel,
        grid=(nj, n // tm),
        in_specs=[
            pl.BlockSpec((tm, N_HEADS * V_HEAD), row),
            pl.BlockSpec((tm, CONV_DIM), row),
            pl.BlockSpec((tm, MEM_DIM), row),
            pl.BlockSpec((tm, D_MODEL), row),
            pl.BlockSpec((N_HEADS * V_HEAD, tn), col),
            pl.BlockSpec((CONV_DIM, tn), col),
            pl.BlockSpec((MEM_DIM, tn), col),
            pl.BlockSpec((D_MODEL, tn), lambda j, i: (0, j)),
            pl.BlockSpec((D_MODEL, tn), lambda j, i: (0, nj + j)),
            pl.BlockSpec((D_MODEL, tn), lambda j, i: (0, 2 * nj + j)),
            pl.BlockSpec((1, tn), lambda j, i: (0, j)),
            pl.BlockSpec((1, tn), lambda j, i: (0, nj + j)),
            pl.BlockSpec((1, tn), lambda j, i: (0, 2 * nj + j)),
        ],
        out_specs=pl.BlockSpec((tm, tn), lambda j, i: (i, j)),
        out_shape=jax.ShapeDtypeStruct((n, D_MODEL), BF16),
        compiler_params=_cparams(("parallel", "parallel")),
        name="branch_mix",
    )(attn, cv, om, xn, w_mla_o, w_conv_o, w_mem_o, wg, wg, wg, b_gate, b_gate, b_gate)


def _out_proj_kernel(mix_ref, wo_ref, x_ref, gffn_ref, h_ref, hn_ref):
    h = x_ref[...] + _dot(mix_ref[...], wo_ref[...])
    h_ref[...] = h
    hn_ref[...] = _rms(h, gffn_ref[...]).astype(BF16)


def _out_proj(mix, w_out, x, g_ffn, tm):
    n = x.shape[0]
    row = lambda i: (i, 0)
    return pl.pallas_call(
        _out_proj_kernel,
        grid=(n // tm,),
        in_specs=[
            pl.BlockSpec((tm, D_MODEL), row),
            pl.BlockSpec((D_MODEL, D_MODEL), lambda i: (0, 0), pipeline_mode=pl.Buffered(1)),
            pl.BlockSpec((tm, D_MODEL), row),
            pl.BlockSpec((1, D_MODEL), lambda i: (0, 0)),
        ],
        out_specs=[pl.BlockSpec((tm, D_MODEL), row), pl.BlockSpec((tm, D_MODEL), row)],
        out_shape=[jax.ShapeDtypeStruct((n, D_MODEL), F32),
                   jax.ShapeDtypeStruct((n, D_MODEL), BF16)],
        compiler_params=_cparams(("parallel",)),
        name="out_proj",
    )(mix, w_out, x, g_ffn)


FFN_HALO = 16
FFN_CHAINS = 2


def _ffn_kernel(hn_ref, halo_ref, hist_ref, wa_ref, wv_ref, wdw_ref, bdw_ref, wd_ref, h_ref,
                gfin_ref, y_ref, abuf, *, tm, t_len):
    i = pl.program_id(0)
    j = pl.program_id(1)

    @pl.when(j == 0)
    def _():
        y_ref[...] = h_ref[...]

    def conv3(a, b0, rows):
        return (wdw_ref[2:3, :] * a
                + wdw_ref[1:2, :] * abuf[b0 - 1:b0 - 1 + rows, :]
                + wdw_ref[0:1, :] * abuf[b0 - 2:b0 - 2 + rows, :]
                + bdw_ref[...])

    nseq = hist_ref.shape[0]
    if nseq == 1:
        a_prev = _dot(halo_ref[...], wa_ref[...])
        at_seq_start = (i * tm) % t_len == 0
        abuf[0:FFN_HALO, :] = jnp.where(at_seq_start, hist_ref[0], a_prev)
        chains = FFN_CHAINS if tm % (FFN_CHAINS * FFN_HALO) == 0 else 1
        rows = tm // chains
        for ch in range(chains):
            r0 = ch * rows
            hn = hn_ref[r0:r0 + rows, :]
            a = _dot(hn, wa_ref[...])
            val = _dot(hn, wv_ref[...])
            abuf[FFN_HALO + r0:FFN_HALO + r0 + rows, :] = a
            conv = conv3(a, FFN_HALO + r0, rows)
            act = (conv * jax.nn.sigmoid(conv) * val).astype(BF16)
            y_ref[r0:r0 + rows, :] += _dot(act, wd_ref[...])
    else:
        seg = tm // nseq
        hn = hn_ref[...]
        a = _dot(hn, wa_ref[...])
        val = _dot(hn, wv_ref[...])
        convs = []
        for s in range(nseq):
            base = s * (seg + FFN_HALO)
            a_s = a[s * seg:(s + 1) * seg, :]
            abuf[base:base + FFN_HALO, :] = hist_ref[s]
            abuf[base + FFN_HALO:base + FFN_HALO + seg, :] = a_s
            convs.append(conv3(a_s, base + FFN_HALO, seg))
        conv = jnp.concatenate(convs, axis=0)
        act = (conv * jax.nn.sigmoid(conv) * val).astype(BF16)
        y_ref[...] += _dot(act, wd_ref[...])

    @pl.when(j == pl.num_programs(1) - 1)
    def _():
        y_ref[...] = _rms(y_ref[...], gfin_ref[...])


def _ffn(hn, hist, w_up, w_dw, b_dw, w_down, h, g_final, t_len, tm, tn):
    n = hn.shape[0]
    nj = D_FF // tn
    r = tm // FFN_HALO
    nseq = max(1, tm // t_len)
    assert tm % t_len == 0 or t_len % tm == 0
    kernel = functools.partial(_ffn_kernel, tm=tm, t_len=t_len)
    return pl.pallas_call(
        kernel,
        grid=(n // tm, nj),
        in_specs=[
            pl.BlockSpec((tm, D_MODEL), lambda i, j: (i, 0)),
            pl.BlockSpec((FFN_HALO, D_MODEL), lambda i, j: (jnp.maximum(i * r - 1, 0), 0)),
            pl.BlockSpec((nseq, FFN_HALO, tn), lambda i, j: ((i * tm) // (t_len * nseq), 0, j)),
            pl.BlockSpec((D_MODEL, tn), lambda i, j: (0, j)),
            pl.BlockSpec((D_MODEL, tn), lambda i, j: (0, nj + j)),
            pl.BlockSpec((FFN_CONV_WIDTH, tn), lambda i, j: (0, j)),
            pl.BlockSpec((1, tn), lambda i, j: (0, j)),
            pl.BlockSpec((tn, D_MODEL), lambda i, j: (j, 0)),
            pl.BlockSpec((tm, D_MODEL), lambda i, j: (i, 0), pipeline_mode=pl.Buffered(1)),
            pl.BlockSpec((1, D_MODEL), lambda i, j: (0, 0)),
        ],
        out_specs=pl.BlockSpec((tm, D_MODEL), lambda i, j: (i, 0)),
        out_shape=jax.ShapeDtypeStruct((n, D_MODEL), F32),
        scratch_shapes=[pltpu.VMEM((tm + nseq * FFN_HALO, tn), F32)],
        compiler_params=_cparams(("parallel", "arbitrary")),
        name="conv_ffn",
    )(hn, hn, hist, w_up, w_up, w_dw, b_dw, w_down, h, g_final)


def _rope_tables(pos):
    half = QK_ROPE // 2
    inv_freq = ROPE_THETA ** (-jnp.arange(half, dtype=F32) / half)
    ang = pos.astype(F32)[:, None] * inv_freq[None, :]
    cos = jnp.cos(ang)
    sin = jnp.sin(ang)
    cos_t = jnp.tile(cos, (1, LANES // half))
    sin_t = jnp.tile(jnp.concatenate([-sin, sin], axis=1), (1, LANES // QK_ROPE))
    return cos_t, sin_t


def _swap_halves(w):
    half = QK_ROPE // 2
    return jnp.concatenate([w[..., half:], w[..., :half]], axis=-1)


def _prep_weights(w_in, w_uq, w_ukv, w_mla_o, w_conv_o, w_mem_o, w_out, w_up, w_down,
                  w_mem_k, w_mem_v, w_conv_dw):
    kr = w_in[:, OFF_KR:OFF_GLU]
    kr_sw = _swap_halves(kr)
    w1 = jnp.concatenate([w_in[:, :OFF_KR], kr, kr, kr_sw, kr_sw], axis=1).astype(BF16)
    w5 = w_in[:, OFF_GLU:OFF_GATE].astype(BF16)
    wg = w_in[:, OFF_GATE:].astype(BF16)
    nope = w_uq[:, :, :QK_NOPE].reshape(Q_LORA, N_HEADS // 2, 2 * QK_NOPE)
    rope = w_uq[:, :, QK_NOPE:]
    rope_p = rope.reshape(Q_LORA, N_HEADS // 2, 2 * QK_ROPE)
    rope_sw = _swap_halves(rope).reshape(Q_LORA, N_HEADS // 2, 2 * QK_ROPE)
    w2 = jnp.concatenate([nope, rope_p, rope_sw], axis=2).reshape(Q_LORA, -1).astype(BF16)
    w3 = jnp.concatenate([w_ukv[:, :, :QK_NOPE].reshape(KV_LORA, -1),
                          w_ukv[:, :, QK_NOPE:].reshape(KV_LORA, -1)], axis=1).astype(BF16)
    wmem = jnp.concatenate([w_mem_k, w_mem_v], axis=1).astype(BF16)
    wdw = jnp.concatenate([w_conv_dw, jnp.zeros((1, CONV_DIM), F32)], axis=0)
    return dict(w1=w1, w5=w5, wg=wg, w2=w2, w3=w3, wmem=wmem, wdw=wdw,
                w_mla_o=w_mla_o.astype(BF16), w_conv_o=w_conv_o.astype(BF16),
                w_mem_o=w_mem_o.astype(BF16), w_out=w_out.astype(BF16),
                w_up=w_up.astype(BF16), w_down=w_down.astype(BF16))


def _pad_hist(hist, rows):
    b, k, c = hist.shape
    return jnp.concatenate([jnp.zeros((b, rows - k, c), hist.dtype), hist], axis=1)


def _encoder_layer(x, past_len, hist_conv, hist_ffn, past_ckv, past_krope, mem_k, mem_v, wts, prm,
                   cfg):
    b, t, _ = x.shape
    n = b * t
    pos = past_len + jnp.arange(t)
    cos_t, sin_t = _rope_tables(pos)
    tm = cfg["tm"]
    if tm > t:
        cos_t = jnp.tile(cos_t, (tm // t, 1))
        sin_t = jnp.tile(sin_t, (tm // t, 1))
    x2 = x.reshape(n, D_MODEL)

    xn, cqn, ckv, ckv_b, krope, krpad = _proj_lat(
        x2, prm["g_mix"], wts["w1"], prm["g_cq"], prm["g_ckv"], cos_t, sin_t, tm)
    if past_len:
        pk = past_krope.astype(BF16)
        zeros = jnp.zeros_like(pk)
        past_pad = jnp.concatenate([pk, zeros, zeros, pk], axis=-1)
        ckv_all = jnp.concatenate([past_ckv.astype(BF16), ckv_b.reshape(b, t, KV_LORA)], axis=1)
        kr_all = jnp.concatenate([past_pad, krpad.reshape(b, t, Q_HEAD_W)], axis=1)
    else:
        ckv_all = ckv_b.reshape(b, t, KV_LORA)
        kr_all = krpad.reshape(b, t, Q_HEAD_W)
    s = past_len + t
    q = _proj_q(cqn, wts["w2"], cos_t, sin_t, tm)
    kf, vf = _proj_kv(ckv_all.reshape(b * s, KV_LORA), kr_all.reshape(b * s, Q_HEAD_W), wts["w3"],
                      _row_tile(b * s, cfg["tm_kv"]))
    attn = _attention(q.reshape(b, t, -1), kf.reshape(b, s, -1), vf.reshape(b, s, -1),
                      past_len, cfg["tq"], cfg["tk"], cfg["hg"])

    om, cv, glu_tail = _glu_conv(xn, wts["w5"], _pad_hist(hist_conv, CONV_HALO), wts["wdw"],
                                 prm["b_conv_dw"], prm["ln_conv_g"], prm["ln_conv_b"], mem_k, mem_v,
                                 t, cfg["tm_glu"])

    mix = _branch_mix(attn.reshape(n, -1), cv.reshape(n, CONV_DIM), om.reshape(n, MEM_DIM), xn,
                      wts["w_mla_o"], wts["w_conv_o"], wts["w_mem_o"], wts["wg"], prm["b_gate"],
                      cfg["tm_mix"], cfg["tn_mix"])
    h, hn = _out_proj(mix, wts["w_out"], x2, prm["g_ffn"], tm)
    y = _ffn(hn, _pad_hist(hist_ffn, FFN_HALO), wts["w_up"], prm["w_ffn_dw"], prm["b_ffn_dw"],
             wts["w_down"], h, prm["g_final"], t, cfg["tm_ffn"], cfg["tn_ffn"])

    new_hist_conv = glu_tail[:, CONV_HALO - (CONV_WIDTH - 1):]
    h_tail = h.reshape(b, t, D_MODEL)[:, t - 8:]
    return (y.reshape(b, t, D_MODEL), ckv.reshape(b, t, KV_LORA), krope.reshape(b, t, QK_ROPE),
            new_hist_conv, h_tail)


def kernel(x_prompt, x_sample, mem_prompt, cache_ckv, cache_krope, state_conv, state_ffn_conv,
           cache_mem_k, cache_mem_v, g_mix, w_in, g_cq, w_uq, g_ckv, w_ukv, w_mla_o,
           w_conv_dw, b_conv_dw, ln_conv_g, ln_conv_b, w_conv_o, g_mem, w_mem_k, w_mem_v,
           w_mem_o, b_gate, w_out, g_ffn, w_up, w_ffn_dw, b_ffn_dw, w_down, g_final):
    depth = g_mix.shape[0]
    assert depth == 1
    l = 0
    bp, tp, _ = x_prompt.shape
    bs, ts, _ = x_sample.shape
    past = cache_ckv.shape[2]
    assert tp >= CONV_HALO and ts >= CONV_HALO

    wts = _prep_weights(w_in[l], w_uq[l], w_ukv[l], w_mla_o[l], w_conv_o[l], w_mem_o[l], w_out[l],
                        w_up[l], w_down[l], w_mem_k[l], w_mem_v[l], w_conv_dw[l])
    row = lambda a: a.reshape(1, -1)
    prm = dict(g_mix=row(g_mix[l]), g_cq=row(g_cq[l]), g_ckv=row(g_ckv[l]),
               b_conv_dw=row(b_conv_dw[l]), ln_conv_g=row(ln_conv_g[l]), ln_conv_b=row(ln_conv_b[l]),
               b_gate=row(b_gate[l]), g_ffn=row(g_ffn[l]), w_ffn_dw=w_ffn_dw[l],
               b_ffn_dw=row(b_ffn_dw[l]), g_final=row(g_final))

    mem2 = mem_prompt.reshape(bp * MEM_LEN, D_MODEL)
    mkv = _norm_matmul(mem2, row(g_mem[l]), wts["wmem"], 256, 512)
    mk_p = mkv[:, :MEM_DIM].reshape(bp, MEM_LEN, MEM_DIM)
    mv_p = mkv[:, MEM_DIM:].reshape(bp, MEM_LEN, MEM_DIM)

    cfg_p = dict(tm=512, tm_kv=512, tq=512, tk=512, hg=16, tm_glu=512,
                 tm_mix=1024, tn_mix=256, tm_ffn=1024, tn_ffn=512)
    n_s = bs * ts
    cfg_s = dict(tm=n_s, tm_kv=512, tq=ts, tk=past + ts, hg=8, tm_glu=ts,
                 tm_mix=n_s, tn_mix=256, tm_ffn=n_s, tn_ffn=512)

    zeros_conv = jnp.zeros((bp, CONV_WIDTH - 1, CONV_DIM), F32)
    zeros_ffn = jnp.zeros((bp, FFN_CONV_WIDTH - 1, D_FF), F32)
    yp, ckv_p, kr_p, cs_p, htail_p = _encoder_layer(
        x_prompt, 0, zeros_conv, zeros_ffn, None, None,
        mk_p.astype(BF16), mv_p.astype(BF16), wts, prm, cfg_p)
    ys, ckv_s, kr_s, cs_s, htail_s = _encoder_layer(
        x_sample, past, state_conv[l], state_ffn_conv[l], cache_ckv[l], cache_krope[l],
        cache_mem_k[l].reshape(bs, MEM_LEN, MEM_DIM).astype(BF16),
        cache_mem_v[l].reshape(bs, MEM_LEN, MEM_DIM).astype(BF16), wts, prm, cfg_s)

    tails = jnp.concatenate([htail_p.reshape(bp * 8, D_MODEL), htail_s.reshape(bs * 8, D_MODEL)], axis=0)
    a_tail = _norm_matmul(tails, prm["g_ffn"], wts["w_up"], tails.shape[0], 512, nout=D_FF)
    fs_p = a_tail[:bp * 8].reshape(bp, 8, D_FF)[:, 8 - (FFN_CONV_WIDTH - 1):]
    fs_s = a_tail[bp * 8:].reshape(bs, 8, D_FF)[:, 8 - (FFN_CONV_WIDTH - 1):]

    st = lambda a: a[None]
    return (yp, ys, st(ckv_p), st(kr_p), st(cs_p), st(fs_p),
            st(mk_p.reshape(bp, MEM_LEN, MEM_HEADS, MEM_HEAD_DIM)),
            st(mv_p.reshape(bp, MEM_LEN, MEM_HEADS, MEM_HEAD_DIM)),
            st(ckv_s), st(kr_s), st(cs_s), st(fs_s))
```

```python
import functools

import numpy as np
import jax
import jax.numpy as jnp
from jax import lax
from jax.experimental import pallas as pl
from jax.experimental.pallas import tpu as pltpu

D_MODEL = 2048
CHUNK = 64
EPS = 1e-6
N_HEADS = 16
QK_NOPE = 128
QK_ROPE = 64
V_HEAD = 128
Q_LORA = 512
KV_LORA = 512
ROPE_THETA = 10000.0
MLA_SCALE = (QK_NOPE + QK_ROPE) ** -0.5
CONV_DIM = 1024
CONV_WIDTH = 31
MEM_LEN = 256
MEM_HEADS = 4
MEM_HEAD_DIM = 256
MEM_DIM = MEM_HEADS * MEM_HEAD_DIM
MEM_SCALE = MEM_HEAD_DIM ** -0.5
N_BRANCH = 3
D_FF = 5632
FFN_CONV_WIDTH = 3
OFF_CKV = Q_LORA
OFF_KR = OFF_CKV + KV_LORA
OFF_GLU = OFF_KR + QK_ROPE
OFF_QM = OFF_GLU + 2 * CONV_DIM
OFF_GATE = OFF_QM + MEM_DIM

LANES = 128
V7X_VMEM_LIMIT = 56 * 1024 * 1024
Q_HEAD_W = 2 * LANES
V_HEAD_W = 2 * LANES
NEG_BIG = -1e30
Q_SCALE = MLA_SCALE * float(np.log2(np.e))

BF16 = jnp.bfloat16
F32 = jnp.float32


def _cparams(sem, vmem=V7X_VMEM_LIMIT):
    return pltpu.CompilerParams(dimension_semantics=sem, vmem_limit_bytes=vmem)


def _row_tile(n, cap, mult=16):
    t = min(cap, n)
    t -= t % mult
    while t > mult and n % t:
        t -= mult
    assert t >= mult and n % t == 0, (n, cap)
    return t


def _rms(x, g):
    y = x * lax.rsqrt(jnp.mean(x * x, axis=-1, keepdims=True) + EPS)
    return y * g


def _dot(a, b):
    return jnp.dot(a, b, preferred_element_type=F32)


def _dot_nt(a, b):
    return lax.dot_general(a, b, (((1,), (1,)), ((), ())), preferred_element_type=F32)


def _proj_lat_kernel(x_ref, gmix_ref, w_ref, gcq_ref, gckv_ref, cos_ref, sin_ref,
                     xn_ref, cqn_ref, ckv_ref, ckvb_ref, kr_ref, krpad_ref):
    xn = _rms(x_ref[...], gmix_ref[...]).astype(BF16)
    xn_ref[...] = xn
    z = _dot(xn, w_ref[...])
    cqn_ref[...] = _rms(z[:, :Q_LORA], gcq_ref[...]).astype(BF16)
    ckv = _rms(z[:, OFF_CKV:OFF_KR], gckv_ref[...])
    ckv_ref[...] = ckv
    ckvb_ref[...] = ckv.astype(BF16)
    r = z[:, OFF_KR:OFF_KR + LANES] * cos_ref[...] + z[:, OFF_KR + LANES:] * sin_ref[...]
    kr_ref[...] = r[:, :QK_ROPE]
    lane = lax.broadcasted_iota(jnp.int32, r.shape, 1)
    zero = jnp.zeros_like(r)
    krpad_ref[:, :LANES] = jnp.where(lane < QK_ROPE, r, zero).astype(BF16)
    krpad_ref[:, LANES:] = jnp.where(lane >= QK_ROPE, r, zero).astype(BF16)


def _proj_lat(x, g_mix, w1, g_cq, g_ckv, cos_t, sin_t, tm):
    n = x.shape[0]
    nt = cos_t.shape[0] // tm
    row = lambda i: (i, 0)
    const = lambda i: (0, 0)
    tab = lambda i: (i % nt, 0)
    return pl.pallas_call(
        _proj_lat_kernel,
        grid=(n // tm,),
        in_specs=[
            pl.BlockSpec((tm, D_MODEL), row),
            pl.BlockSpec((1, D_MODEL), const),
            pl.BlockSpec(w1.shape, const),
            pl.BlockSpec((1, Q_LORA), const),
            pl.BlockSpec((1, KV_LORA), const),
            pl.BlockSpec((tm, LANES), tab),
            pl.BlockSpec((tm, LANES), tab),
        ],
        out_specs=[
            pl.BlockSpec((tm, D_MODEL), row),
            pl.BlockSpec((tm, Q_LORA), row),
            pl.BlockSpec((tm, KV_LORA), row),
            pl.BlockSpec((tm, KV_LORA), row),
            pl.BlockSpec((tm, QK_ROPE), row),
            pl.BlockSpec((tm, Q_HEAD_W), row),
        ],
        out_shape=[
            jax.ShapeDtypeStruct((n, D_MODEL), BF16),
            jax.ShapeDtypeStruct((n, Q_LORA), BF16),
            jax.ShapeDtypeStruct((n, KV_LORA), F32),
            jax.ShapeDtypeStruct((n, KV_LORA), BF16),
            jax.ShapeDtypeStruct((n, QK_ROPE), F32),
            jax.ShapeDtypeStruct((n, Q_HEAD_W), BF16),
        ],
        compiler_params=_cparams(("parallel",)),
        name="proj_lat",
    )(x, g_mix, w1, g_cq, g_ckv, cos_t, sin_t)


def _proj_q_kernel(cqn_ref, w_ref, cos_ref, sin_ref, q_ref):
    cqn = cqn_ref[...]
    cos = cos_ref[...]
    sin = sin_ref[...]
    pw = 4 * LANES
    for p in range(N_HEADS // 2):
        r = _dot(cqn, w_ref[:, p * pw:(p + 1) * pw])
        rp = ((r[:, 2 * LANES:3 * LANES] * cos + r[:, 3 * LANES:] * sin) * Q_SCALE).astype(BF16)
        q_ref[:, p * pw:p * pw + LANES] = (r[:, :LANES] * Q_SCALE).astype(BF16)
        q_ref[:, p * pw + LANES:p * pw + 2 * LANES] = rp
        q_ref[:, p * pw + 2 * LANES:p * pw + 3 * LANES] = (r[:, LANES:2 * LANES] * Q_SCALE).astype(BF16)
        q_ref[:, p * pw + 3 * LANES:(p + 1) * pw] = rp


def _proj_q(cqn, w2, cos_t, sin_t, tm):
    n = cqn.shape[0]
    nt = cos_t.shape[0] // tm
    return pl.pallas_call(
        _proj_q_kernel,
        grid=(n // tm,),
        in_specs=[
            pl.BlockSpec((tm, Q_LORA), lambda i: (i, 0)),
            pl.BlockSpec(w2.shape, lambda i: (0, 0)),
            pl.BlockSpec((tm, LANES), lambda i: (i % nt, 0)),
            pl.BlockSpec((tm, LANES), lambda i: (i % nt, 0)),
        ],
        out_specs=pl.BlockSpec((tm, N_HEADS * Q_HEAD_W), lambda i: (i, 0)),
        out_shape=jax.ShapeDtypeStruct((n, N_HEADS * Q_HEAD_W), BF16),
        compiler_params=_cparams(("parallel",)),
        name="proj_q",
    )(cqn, w2, cos_t, sin_t)


def _proj_kv_kernel(c_ref, kr_ref, w_ref, k_ref, v_ref):
    c = c_ref[...]
    hw = N_HEADS * QK_NOPE
    kn = _dot(c, w_ref[:, :hw]).astype(BF16)
    vv = _dot(c, w_ref[:, hw:]).astype(BF16)
    ones = jnp.ones((c.shape[0], LANES), BF16)
    for h in range(N_HEADS):
        par = h % 2
        k_ref[:, h * Q_HEAD_W:h * Q_HEAD_W + QK_NOPE] = kn[:, h * QK_NOPE:(h + 1) * QK_NOPE]
        k_ref[:, h * Q_HEAD_W + QK_NOPE:(h + 1) * Q_HEAD_W] = kr_ref[:, par * LANES:(par + 1) * LANES]
        v_ref[:, h * V_HEAD_W:h * V_HEAD_W + V_HEAD] = vv[:, h * V_HEAD:(h + 1) * V_HEAD]
        v_ref[:, h * V_HEAD_W + V_HEAD:(h + 1) * V_HEAD_W] = ones


def _proj_kv(ckv_b, krpad, w3, tm):
    n = ckv_b.shape[0]
    return pl.pallas_call(
        _proj_kv_kernel,
        grid=(n // tm,),
        in_specs=[
            pl.BlockSpec((tm, KV_LORA), lambda i: (i, 0)),
            pl.BlockSpec((tm, Q_HEAD_W), lambda i: (i, 0)),
            pl.BlockSpec(w3.shape, lambda i: (0, 0)),
        ],
        out_specs=[pl.BlockSpec((tm, N_HEADS * Q_HEAD_W), lambda i: (i, 0)),
                   pl.BlockSpec((tm, N_HEADS * V_HEAD_W), lambda i: (i, 0))],
        out_shape=[jax.ShapeDtypeStruct((n, N_HEADS * Q_HEAD_W), BF16),
                   jax.ShapeDtypeStruct((n, N_HEADS * V_HEAD_W), BF16)],
        compiler_params=_cparams(("parallel",)),
        name="proj_kv",
    )(ckv_b, krpad, w3)


def _attn_schedule(t, s, p, tq, tk):
    qi_l, ki_l, fl_l = [], [], []
    for qi in range(t // tq):
        q_lo = p + qi * tq
        q_hi = p + (qi + 1) * tq - 1
        kmax = min(s, (q_hi // CHUNK + 1) * CHUNK)
        full_vis = (q_lo // CHUNK + 1) * CHUNK
        nk = -(-kmax // tk)
        for ki in range(nk):
            need_mask = (ki + 1) * tk > full_vis
            qi_l.append(qi)
            ki_l.append(ki)
            fl_l.append(int(ki == 0) | (int(ki == nk - 1) << 1) | (int(need_mask) << 2))
    return (np.asarray(qi_l, np.int32), np.asarray(ki_l, np.int32), np.asarray(fl_l, np.int32))


def _attn_kernel(qi_tab, ki_tab, fl_tab, q_ref, k_ref, v_ref, o_ref, m_scr, acc_scr,
                 *, hg, tq, tk, past):
    step = pl.program_id(2)
    flags = fl_tab[step]
    qi = qi_tab[step]
    ki = ki_tab[step]

    @pl.when((flags & 1) != 0)
    def _():
        m_scr[...] = jnp.full(m_scr.shape, NEG_BIG, F32)
        acc_scr[...] = jnp.zeros(acc_scr.shape, F32)

    def body(masked):
        if masked:
            qpos = past + qi * tq + lax.broadcasted_iota(jnp.int32, (tq, tk), 0)
            kpos = ki * tk + lax.broadcasted_iota(jnp.int32, (tq, tk), 1)
            visible = (kpos // CHUNK) <= (qpos // CHUNK)
        for h in range(hg):
            q = q_ref[0, :, h * Q_HEAD_W:(h + 1) * Q_HEAD_W]
            k = k_ref[0, :, h * Q_HEAD_W:(h + 1) * Q_HEAD_W]
            sc = _dot_nt(q, k)
            if masked:
                sc = jnp.where(visible, sc, NEG_BIG)
            m_prev = m_scr[h]
            m_cur = jnp.max(sc, axis=1, keepdims=True)
            m_next = jnp.maximum(m_prev, m_cur)
            if tk % LANES == 0:
                p = jnp.exp2(sc - jnp.concatenate([m_next] * (tk // LANES), axis=1))
            else:
                p = jnp.exp2(sc - m_next[:, :1])
            alpha = jnp.exp2(m_prev - m_next)
            m_scr[h] = m_next
            pv = _dot(p.astype(BF16), v_ref[0, :, h * V_HEAD_W:(h + 1) * V_HEAD_W])
            acc_scr[h] = acc_scr[h] * jnp.concatenate([alpha] * (V_HEAD_W // LANES), axis=1) + pv

    @pl.when((flags & 4) != 0)
    def _():
        body(True)

    @pl.when((flags & 4) == 0)
    def _():
        body(False)

    @pl.when((flags & 2) != 0)
    def _():
        for h in range(hg):
            acc = acc_scr[h]
            o_ref[0, :, h * V_HEAD:(h + 1) * V_HEAD] = (acc[:, :V_HEAD] / acc[:, V_HEAD:]).astype(o_ref.dtype)


def _attention(q, k, v, past, tq, tk, hg):
    b, t, _ = q.shape
    s = k.shape[1]
    qi_np, ki_np, fl_np = _attn_schedule(t, s, past, tq, tk)
    n_steps = int(qi_np.shape[0])
    kernel = functools.partial(_attn_kernel, hg=hg, tq=tq, tk=tk, past=past)
    grid_spec = pltpu.PrefetchScalarGridSpec(
        num_scalar_prefetch=3,
        grid=(b, N_HEADS // hg, n_steps),
        in_specs=[
            pl.BlockSpec((1, tq, hg * Q_HEAD_W), lambda bi, g, st, qt, kt, ft: (bi, qt[st], g)),
            pl.BlockSpec((1, tk, hg * Q_HEAD_W), lambda bi, g, st, qt, kt, ft: (bi, kt[st], g)),
            pl.BlockSpec((1, tk, hg * V_HEAD_W), lambda bi, g, st, qt, kt, ft: (bi, kt[st], g)),
        ],
        out_specs=pl.BlockSpec((1, tq, hg * V_HEAD), lambda bi, g, st, qt, kt, ft: (bi, qt[st], g)),
        scratch_shapes=[
            pltpu.VMEM((hg, tq, LANES), F32),
            pltpu.VMEM((hg, tq, V_HEAD_W), F32),
        ],
    )
    return pl.pallas_call(
        kernel,
        grid_spec=grid_spec,
        out_shape=jax.ShapeDtypeStruct((b, t, N_HEADS * V_HEAD), BF16),
        compiler_params=_cparams(("parallel", "parallel", "arbitrary")),
        name="mla_attention",
    )(jnp.asarray(qi_np), jnp.asarray(ki_np), jnp.asarray(fl_np), q, k, v)


CONV_HALO = 32
SUBLANES = 8
CONV_ROWS = 32
GLU_CHAINS = 1


def _glu_conv_kernel(xn_ref, w_ref, hist_ref, wdw_ref, b_ref, g_ref, beta_ref, mk_ref, mv_ref,
                     om_ref, cv_ref, tail_ref, carry, wb, *chain_scratch, tm, t_len):
    i = pl.program_id(0)
    gbufs, shifteds, cbufs = chain_scratch[0::3], chain_scratch[1::3], chain_scratch[2::3]
    chains = len(gbufs)
    rows = tm // chains

    @pl.when((i * tm) % t_len == 0)
    def _():
        carry[...] = hist_ref[0]

    for k in range(CONV_WIDTH):
        wb[k] = jnp.broadcast_to(wdw_ref[k:k + 1, :], (SUBLANES, CONV_DIM))
    wb[CONV_WIDTH] = jnp.broadcast_to(b_ref[...], (SUBLANES, CONV_DIM))

    off = CONV_HALO - (CONV_WIDTH - 1)
    groups = CONV_ROWS // SUBLANES
    for ch in range(chains):
        r0 = ch * rows
        gbuf, shifted, cbuf = gbufs[ch], shifteds[ch], cbufs[ch]
        gbuf[0:CONV_HALO, :] = carry[...] if ch == 0 else gbufs[ch - 1][rows:rows + CONV_HALO, :]
        xn = xn_ref[r0:r0 + rows, :]
        a = _dot(xn, w_ref[:, :CONV_DIM])
        bgate = _dot(xn, w_ref[:, CONV_DIM:2 * CONV_DIM])
        gbuf[CONV_HALO:, :] = a * jax.nn.sigmoid(bgate)
        qm = _dot(xn, w_ref[:, 2 * CONV_DIM:]).astype(BF16)
        for h in range(MEM_HEADS):
            sl = slice(h * MEM_HEAD_DIM, (h + 1) * MEM_HEAD_DIM)
            sc = _dot_nt(qm[:, sl], mk_ref[0, :, sl]) * MEM_SCALE
            e = jnp.exp(sc - jnp.max(sc, axis=-1, keepdims=True))
            o = _dot(e.astype(BF16), mv_ref[0, :, sl])
            om_ref[r0:r0 + rows, sl] = (o / jnp.sum(e, axis=-1, keepdims=True)).astype(om_ref.dtype)
        span = rows + CONV_HALO - SUBLANES
        for p in range(1, SUBLANES):
            shifted[p - 1] = gbuf[p:p + span, :]
        for c0 in range(0, rows, CONV_ROWS):
            accs = [wb[CONV_WIDTH]] * groups
            for k in range(CONV_WIDTH):
                s = off + k + c0
                p = s % SUBLANES
                wk = wb[k]
                for g in range(groups):
                    if p == 0:
                        rws = gbuf[s + g * SUBLANES:s + (g + 1) * SUBLANES, :]
                    else:
                        rws = shifted[p - 1, s - p + g * SUBLANES:s - p + (g + 1) * SUBLANES, :]
                    accs[g] = accs[g] + wk * rws
            for g in range(groups):
                cbuf[c0 + g * SUBLANES:c0 + (g + 1) * SUBLANES, :] = accs[g]
        cv = cbuf[...]
        mu = jnp.mean(cv, axis=-1, keepdims=True)
        d = cv - mu
        var = jnp.mean(d * d, axis=-1, keepdims=True)
        y = d * lax.rsqrt(var + EPS) * g_ref[...] + beta_ref[...]
        cv_ref[r0:r0 + rows, :] = (y * jax.nn.sigmoid(y)).astype(cv_ref.dtype)

    last = gbufs[-1][rows:rows + CONV_HALO, :]
    tail_ref[0] = last
    carry[...] = last


def _glu_conv(xn, w5, hist, w_dw, b_dw, ln_g, ln_b, mem_k, mem_v, t_len, tm):
    n = xn.shape[0]
    assert t_len % tm == 0 and tm % CONV_ROWS == 0 and tm >= CONV_HALO
    nt = t_len // tm
    chains = GLU_CHAINS if tm % (GLU_CHAINS * CONV_ROWS) == 0 else 1
    rows = tm // chains
    chain_scratch = [pltpu.VMEM((rows + CONV_HALO, CONV_DIM), F32),
                     pltpu.VMEM((SUBLANES - 1, rows + CONV_HALO - SUBLANES, CONV_DIM), F32),
                     pltpu.VMEM((rows, CONV_DIM), F32)] * chains
    kernel = functools.partial(_glu_conv_kernel, tm=tm, t_len=t_len)
    const = lambda i: (0, 0)
    return pl.pallas_call(
        kernel,
        grid=(n // tm,),
        in_specs=[
            pl.BlockSpec((tm, D_MODEL), lambda i: (i, 0)),
            pl.BlockSpec(w5.shape, const, pipeline_mode=pl.Buffered(1)),
            pl.BlockSpec((1, CONV_HALO, CONV_DIM), lambda i: (i // nt, 0, 0)),
            pl.BlockSpec(w_dw.shape, const),
            pl.BlockSpec((1, CONV_DIM), const),
            pl.BlockSpec((1, CONV_DIM), const),
            pl.BlockSpec((1, CONV_DIM), const),
            pl.BlockSpec((1, MEM_LEN, MEM_DIM), lambda i: (i // nt, 0, 0)),
            pl.BlockSpec((1, MEM_LEN, MEM_DIM), lambda i: (i // nt, 0, 0)),
        ],
        out_specs=[pl.BlockSpec((tm, MEM_DIM), lambda i: (i, 0)),
                   pl.BlockSpec((tm, CONV_DIM), lambda i: (i, 0)),
                   pl.BlockSpec((1, CONV_HALO, CONV_DIM), lambda i: (i // nt, 0, 0))],
        out_shape=[jax.ShapeDtypeStruct((n, MEM_DIM), BF16),
                   jax.ShapeDtypeStruct((n, CONV_DIM), BF16),
                   jax.ShapeDtypeStruct((n // t_len, CONV_HALO, CONV_DIM), F32)],
        scratch_shapes=[pltpu.VMEM((CONV_HALO, CONV_DIM), F32),
                        pltpu.VMEM((CONV_WIDTH + 1, SUBLANES, CONV_DIM), F32)] + chain_scratch,
        compiler_params=_cparams(("arbitrary",)),
        name="glu_conv",
    )(xn, w5, hist, w_dw, b_dw, ln_g, ln_b, mem_k, mem_v)


def _norm_matmul_kernel(x_ref, g_ref, w_ref, o_ref):
    xn = _rms(x_ref[...], g_ref[...]).astype(BF16)
    o_ref[...] = _dot(xn, w_ref[...])


def _norm_matmul(x, g, w, tm, tn, nout=None):
    n, kdim = x.shape
    nout = w.shape[1] if nout is None else nout
    return pl.pallas_call(
        _norm_matmul_kernel,
        grid=(n // tm, nout // tn),
        in_specs=[
            pl.BlockSpec((tm, kdim), lambda i, j: (i, 0)),
            pl.BlockSpec((1, kdim), lambda i, j: (0, 0)),
            pl.BlockSpec((kdim, tn), lambda i, j: (0, j)),
        ],
        out_specs=pl.BlockSpec((tm, tn), lambda i, j: (i, j)),
        out_shape=jax.ShapeDtypeStruct((n, nout), F32),
        compiler_params=_cparams(("parallel", "arbitrary")),
        name="norm_matmul",
    )(x, g, w)


MIX_CHAINS = 4


def _mix_kernel(attn_ref, cv_ref, om_ref, xn_ref, wa_ref, wb_ref, wc_ref,
                wg0_ref, wg1_ref, wg2_ref, bg0_ref, bg1_ref, bg2_ref, mix_ref):
    tm = mix_ref.shape[0]
    rows = tm // MIX_CHAINS
    for ch in range(MIX_CHAINS):
        sl = slice(ch * rows, (ch + 1) * rows)
        xn = xn_ref[sl, :]
        a = _dot(attn_ref[sl, :], wa_ref[...])
        bo = _dot(cv_ref[sl, :], wb_ref[...])
        c = _dot(om_ref[sl, :], wc_ref[...])
        g0 = jax.nn.sigmoid(_dot(xn, wg0_ref[...]) + bg0_ref[...])
        g1 = jax.nn.sigmoid(_dot(xn, wg1_ref[...]) + bg1_ref[...])
        g2 = jax.nn.sigmoid(_dot(xn, wg2_ref[...]) + bg2_ref[...])
        mix_ref[sl, :] = (g0 * a + g1 * bo + g2 * c).astype(BF16)


def _branch_mix(attn, cv, om, xn, w_mla_o, w_conv_o, w_mem_o, wg, b_gate, tm, tn):
    n = xn.shape[0]
    nj = D_MODEL // tn
    assert wg.shape[1] == N_BRANCH * D_MODEL and b_gate.shape[1] == N_BRANCH * D_MODEL
    row = lambda j, i: (i, 0)
    col = lambda j, i: (0, j)
    return pl.pallas_call(
        _mix_kernel,
        grid=(nj, n // tm),
        in_specs=[
            pl.BlockSpec((tm, N_HEADS * V_HEAD), row),
            pl.BlockSpec((tm, CONV_DIM), row),
            pl.BlockSpec((tm, MEM_DIM), row),
            pl.BlockSpec((tm, D_MODEL), row),
            pl.BlockSpec((N_HEADS * V_HEAD, tn), col),
            pl.BlockSpec((CONV_DIM, tn), col),
            pl.BlockSpec((MEM_DIM, tn), col),
            pl.BlockSpec((D_MODEL, tn), lambda j, i: (0, j)),
            pl.BlockSpec((D_MODEL, tn), lambda j, i: (0, nj + j)),
            pl.BlockSpec((D_MODEL, tn), lambda j, i: (0, 2 * nj + j)),
            pl.BlockSpec((1, tn), lambda j, i: (0, j)),
            pl.BlockSpec((1, tn), lambda j, i: (0, nj + j)),
            pl.BlockSpec((1, tn), lambda j, i: (0, 2 * nj + j)),
        ],
        out_specs=pl.BlockSpec((tm, tn), lambda j, i: (i, j)),
        out_shape=jax.ShapeDtypeStruct((n, D_MODEL), BF16),
        compiler_params=_cparams(("parallel", "parallel")),
        name="branch_mix",
    )(attn, cv, om, xn, w_mla_o, w_conv_o, w_mem_o, wg, wg, wg, b_gate, b_gate, b_gate)


def _out_proj_kernel(mix_ref, wo_ref, x_ref, gffn_ref, h_ref, hn_ref):
    h = x_ref[...] + _dot(mix_ref[...], wo_ref[...])
    h_ref[...] = h
    hn_ref[...] = _rms(h, gffn_ref[...]).astype(BF16)


def _out_proj(mix, w_out, x, g_ffn, tm):
    n = x.shape[0]
    row = lambda i: (i, 0)
    return pl.pallas_call(
        _out_proj_kernel,
        grid=(n // tm,),
        in_specs=[
            pl.BlockSpec((tm, D_MODEL), row),
            pl.BlockSpec((D_MODEL, D_MODEL), lambda i: (0, 0), pipeline_mode=pl.Buffered(1)),
            pl.BlockSpec((tm, D_MODEL), row),
            pl.BlockSpec((1, D_MODEL), lambda i: (0, 0)),
        ],
        out_specs=[pl.BlockSpec((tm, D_MODEL), row), pl.BlockSpec((tm, D_MODEL), row)],
        out_shape=[jax.ShapeDtypeStruct((n, D_MODEL), F32),
                   jax.ShapeDtypeStruct((n, D_MODEL), BF16)],
        compiler_params=_cparams(("parallel",)),
        name="out_proj",
    )(mix, w_out, x, g_ffn)


FFN_HALO = 16
FFN_CHAINS = 2


def _ffn_kernel(hn_ref, halo_ref, hist_ref, wa_ref, wv_ref, wdw_ref, bdw_ref, wd_ref, h_ref,
                gfin_ref, y_ref, abuf, *, tm, t_len):
    i = pl.program_id(0)
    j = pl.program_id(1)

    @pl.when(j == 0)
    def _():
        y_ref[...] = h_ref[...]

    def conv3(a, b0, rows):
        return (wdw_ref[2:3, :] * a
                + wdw_ref[1:2, :] * abuf[b0 - 1:b0 - 1 + rows, :]
                + wdw_ref[0:1, :] * abuf[b0 - 2:b0 - 2 + rows, :]
                + bdw_ref[...])

    nseq = hist_ref.shape[0]
    if nseq == 1:
        a_prev = _dot(halo_ref[...], wa_ref[...])
        at_seq_start = (i * tm) % t_len == 0
        abuf[0:FFN_HALO, :] = jnp.where(at_seq_start, hist_ref[0], a_prev)
        chains = FFN_CHAINS if tm % (FFN_CHAINS * FFN_HALO) == 0 else 1
        rows = tm // chains
        for ch in range(chains):
            r0 = ch * rows
            hn = hn_ref[r0:r0 + rows, :]
            a = _dot(hn, wa_ref[...])
            val = _dot(hn, wv_ref[...])
            abuf[FFN_HALO + r0:FFN_HALO + r0 + rows, :] = a
            conv = conv3(a, FFN_HALO + r0, rows)
            act = (conv * jax.nn.sigmoid(conv) * val).astype(BF16)
            y_ref[r0:r0 + rows, :] += _dot(act, wd_ref[...])
    else:
        seg = tm // nseq
        hn = hn_ref[...]
        a = _dot(hn, wa_ref[...])
        val = _dot(hn, wv_ref[...])
        convs = []
        for s in range(nseq):
            base = s * (seg + FFN_HALO)
            a_s = a[s * seg:(s + 1) * seg, :]
            abuf[base:base + FFN_HALO, :] = hist_ref[s]
            abuf[base + FFN_HALO:base + FFN_HALO + seg, :] = a_s
            convs.append(conv3(a_s, base + FFN_HALO, seg))
        conv = jnp.concatenate(convs, axis=0)
        act = (conv * jax.nn.sigmoid(conv) * val).astype(BF16)
        y_ref[...] += _dot(act, wd_ref[...])

    @pl.when(j == pl.num_programs(1) - 1)
    def _():
        y_ref[...] = _rms(y_ref[...], gfin_ref[...])


def _ffn(hn, hist, w_up, w_dw, b_dw, w_down, h, g_final, t_len, tm, tn):
    n = hn.shape[0]
    nj = D_FF // tn
    r = tm // FFN_HALO
    nseq = max(1, tm // t_len)
    assert tm % t_len == 0 or t_len % tm == 0
    kernel = functools.partial(_ffn_kernel, tm=tm, t_len=t_len)
    return pl.pallas_call(
        kernel,
        grid=(n // tm, nj),
        in_specs=[
            pl.BlockSpec((tm, D_MODEL), lambda i, j: (i, 0)),
            pl.BlockSpec((FFN_HALO, D_MODEL), lambda i, j: (jnp.maximum(i * r - 1, 0), 0)),
            pl.BlockSpec((nseq, FFN_HALO, tn), lambda i, j: ((i * tm) // (t_len * nseq), 0, j)),
            pl.BlockSpec((D_MODEL, tn), lambda i, j: (0, j)),
            pl.BlockSpec((D_MODEL, tn), lambda i, j: (0, nj + j)),
            pl.BlockSpec((FFN_CONV_WIDTH, tn), lambda i, j: (0, j)),
            pl.BlockSpec((1, tn), lambda i, j: (0, j)),
            pl.BlockSpec((tn, D_MODEL), lambda i, j: (j, 0)),
            pl.BlockSpec((tm, D_MODEL), lambda i, j: (i, 0), pipeline_mode=pl.Buffered(1)),
            pl.BlockSpec((1, D_MODEL), lambda i, j: (0, 0)),
        ],
        out_specs=pl.BlockSpec((tm, D_MODEL), lambda i, j: (i, 0)),
        out_shape=jax.ShapeDtypeStruct((n, D_MODEL), F32),
        scratch_shapes=[pltpu.VMEM((tm + nseq * FFN_HALO, tn), F32)],
        compiler_params=_cparams(("parallel", "arbitrary")),
        name="conv_ffn",
    )(hn, hn, hist, w_up, w_up, w_dw, b_dw, w_down, h, g_final)


def _rope_tables(pos):
    half = QK_ROPE // 2
    inv_freq = ROPE_THETA ** (-jnp.arange(half, dtype=F32) / half)
    ang = pos.astype(F32)[:, None] * inv_freq[None, :]
    cos = jnp.cos(ang)
    sin = jnp.sin(ang)
    cos_t = jnp.tile(cos, (1, LANES // half))
    sin_t = jnp.tile(jnp.concatenate([-sin, sin], axis=1), (1, LANES // QK_ROPE))
    return cos_t, sin_t


def _swap_halves(w):
    half = QK_ROPE // 2
    return jnp.concatenate([w[..., half:], w[..., :half]], axis=-1)


def _prep_weights(w_in, w_uq, w_ukv, w_mla_o, w_conv_o, w_mem_o, w_out, w_up, w_down,
                  w_mem_k, w_mem_v, w_conv_dw):
    kr = w_in[:, OFF_KR:OFF_GLU]
    kr_sw = _swap_halves(kr)
    w1 = jnp.concatenate([w_in[:, :OFF_KR], kr, kr, kr_sw, kr_sw], axis=1).astype(BF16)
    w5 = w_in[:, OFF_GLU:OFF_GATE].astype(BF16)
    wg = w_in[:, OFF_GATE:].astype(BF16)
    nope = w_uq[:, :, :QK_NOPE].reshape(Q_LORA, N_HEADS // 2, 2 * QK_NOPE)
    rope = w_uq[:, :, QK_NOPE:]
    rope_p = rope.reshape(Q_LORA, N_HEADS // 2, 2 * QK_ROPE)
    rope_sw = _swap_halves(rope).reshape(Q_LORA, N_HEADS // 2, 2 * QK_ROPE)
    w2 = jnp.concatenate([nope, rope_p, rope_sw], axis=2).reshape(Q_LORA, -1).astype(BF16)
    w3 = jnp.concatenate([w_ukv[:, :, :QK_NOPE].reshape(KV_LORA, -1),
                          w_ukv[:, :, QK_NOPE:].reshape(KV_LORA, -1)], axis=1).astype(BF16)
    wmem = jnp.concatenate([w_mem_k, w_mem_v], axis=1).astype(BF16)
    wdw = jnp.concatenate([w_conv_dw, jnp.zeros((1, CONV_DIM), F32)], axis=0)
    return dict(w1=w1, w5=w5, wg=wg, w2=w2, w3=w3, wmem=wmem, wdw=wdw,
                w_mla_o=w_mla_o.astype(BF16), w_conv_o=w_conv_o.astype(BF16),
                w_mem_o=w_mem_o.astype(BF16), w_out=w_out.astype(BF16),
                w_up=w_up.astype(BF16), w_down=w_down.astype(BF16))


def _pad_hist(hist, rows):
    b, k, c = hist.shape
    return jnp.concatenate([jnp.zeros((b, rows - k, c), hist.dtype), hist], axis=1)


def _encoder_layer(x, past_len, hist_conv, hist_ffn, past_ckv, past_krope, mem_k, mem_v, wts, prm,
                   cfg):
    b, t, _ = x.shape
    n = b * t
    pos = past_len + jnp.arange(t)
    cos_t, sin_t = _rope_tables(pos)
    tm = cfg["tm"]
    if tm > t:
        cos_t = jnp.tile(cos_t, (tm // t, 1))
        sin_t = jnp.tile(sin_t, (tm // t, 1))
    x2 = x.reshape(n, D_MODEL)

    xn, cqn, ckv, ckv_b, krope, krpad = _proj_lat(
        x2, prm["g_mix"], wts["w1"], prm["g_cq"], prm["g_ckv"], cos_t, sin_t, tm)
    if past_len:
        pk = past_krope.astype(BF16)
        zeros = jnp.zeros_like(pk)
        past_pad = jnp.concatenate([pk, zeros, zeros, pk], axis=-1)
        ckv_all = jnp.concatenate([past_ckv.astype(BF16), ckv_b.reshape(b, t, KV_LORA)], axis=1)
        kr_all = jnp.concatenate([past_pad, krpad.reshape(b, t, Q_HEAD_W)], axis=1)
    else:
        ckv_all = ckv_b.reshape(b, t, KV_LORA)
        kr_all = krpad.reshape(b, t, Q_HEAD_W)
    s = past_len + t
    q = _proj_q(cqn, wts["w2"], cos_t, sin_t, tm)
    kf, vf = _proj_kv(ckv_all.reshape(b * s, KV_LORA), kr_all.reshape(b * s, Q_HEAD_W), wts["w3"],
                      _row_tile(b * s, cfg["tm_kv"]))
    attn = _attention(q.reshape(b, t, -1), kf.reshape(b, s, -1), vf.reshape(b, s, -1),
                      past_len, cfg["tq"], cfg["tk"], cfg["hg"])

    om, cv, glu_tail = _glu_conv(xn, wts["w5"], _pad_hist(hist_conv, CONV_HALO), wts["wdw"],
                                 prm["b_conv_dw"], prm["ln_conv_g"], prm["ln_conv_b"], mem_k, mem_v,
                                 t, cfg["tm_glu"])

    mix = _branch_mix(attn.reshape(n, -1), cv.reshape(n, CONV_DIM), om.reshape(n, MEM_DIM), xn,
                      wts["w_mla_o"], wts["w_conv_o"], wts["w_mem_o"], wts["wg"], prm["b_gate"],
                      cfg["tm_mix"], cfg["tn_mix"])
    h, hn = _out_proj(mix, wts["w_out"], x2, prm["g_ffn"], tm)
    y = _ffn(hn, _pad_hist(hist_ffn, FFN_HALO), wts["w_up"], prm["w_ffn_dw"], prm["b_ffn_dw"],
             wts["w_down"], h, prm["g_final"], t, cfg["tm_ffn"], cfg["tn_ffn"])

    new_hist_conv = glu_tail[:, CONV_HALO - (CONV_WIDTH - 1):]
    h_tail = h.reshape(b, t, D_MODEL)[:, t - 8:]
    return (y.reshape(b, t, D_MODEL), ckv.reshape(b, t, KV_LORA), krope.reshape(b, t, QK_ROPE),
            new_hist_conv, h_tail)


def kernel(x_prompt, x_sample, mem_prompt, cache_ckv, cache_krope, state_conv, state_ffn_conv,
           cache_mem_k, cache_mem_v, g_mix, w_in, g_cq, w_uq, g_ckv, w_ukv, w_mla_o,
           w_conv_dw, b_conv_dw, ln_conv_g, ln_conv_b, w_conv_o, g_mem, w_mem_k, w_mem_v,
           w_mem_o, b_gate, w_out, g_ffn, w_up, w_ffn_dw, b_ffn_dw, w_down, g_final):
    depth = g_mix.shape[0]
    assert depth == 1
    l = 0
    bp, tp, _ = x_prompt.shape
    bs, ts, _ = x_sample.shape
    past = cache_ckv.shape[2]
    assert tp >= CONV_HALO and ts >= CONV_HALO

    wts = _prep_weights(w_in[l], w_uq[l], w_ukv[l], w_mla_o[l], w_conv_o[l], w_mem_o[l], w_out[l],
                        w_up[l], w_down[l], w_mem_k[l], w_mem_v[l], w_conv_dw[l])
    row = lambda a: a.reshape(1, -1)
    prm = dict(g_mix=row(g_mix[l]), g_cq=row(g_cq[l]), g_ckv=row(g_ckv[l]),
               b_conv_dw=row(b_conv_dw[l]), ln_conv_g=row(ln_conv_g[l]), ln_conv_b=row(ln_conv_b[l]),
               b_gate=row(b_gate[l]), g_ffn=row(g_ffn[l]), w_ffn_dw=w_ffn_dw[l],
               b_ffn_dw=row(b_ffn_dw[l]), g_final=row(g_final))

    mem2 = mem_prompt.reshape(bp * MEM_LEN, D_MODEL)
    mkv = _norm_matmul(mem2, row(g_mem[l]), wts["wmem"], 256, 512)
    mk_p = mkv[:, :MEM_DIM].reshape(bp, MEM_LEN, MEM_DIM)
    mv_p = mkv[:, MEM_DIM:].reshape(bp, MEM_LEN, MEM_DIM)

    cfg_p = dict(tm=512, tm_kv=512, tq=512, tk=512, hg=16, tm_glu=512,
                 tm_mix=1024, tn_mix=256, tm_ffn=1024, tn_ffn=512)
    n_s = bs * ts
    cfg_s = dict(tm=n_s, tm_kv=512, tq=ts, tk=past + ts, hg=8, tm_glu=ts,
                 tm_mix=n_s, tn_mix=256, tm_ffn=n_s, tn_ffn=512)

    zeros_conv = jnp.zeros((bp, CONV_WIDTH - 1, CONV_DIM), F32)
    zeros_ffn = jnp.zeros((bp, FFN_CONV_WIDTH - 1, D_FF), F32)
    yp, ckv_p, kr_p, cs_p, htail_p = _encoder_layer(
        x_prompt, 0, zeros_conv, zeros_ffn, None, None,
        mk_p.astype(BF16), mv_p.astype(BF16), wts, prm, cfg_p)
    ys, ckv_s, kr_s, cs_s, htail_s = _encoder_layer(
        x_sample, past, state_conv[l], state_ffn_conv[l], cache_ckv[l], cache_krope[l],
        cache_mem_k[l].reshape(bs, MEM_LEN, MEM_DIM).astype(BF16),
        cache_mem_v[l].reshape(bs, MEM_LEN, MEM_DIM).astype(BF16), wts, prm, cfg_s)

    tails = jnp.concatenate([htail_p.reshape(bp * 8, D_MODEL), htail_s.reshape(bs * 8, D_MODEL)], axis=0)
    a_tail = _norm_matmul(tails, prm["g_ffn"], wts["w_up"], tails.shape[0], 512, nout=D_FF)
    fs_p = a_tail[:bp * 8].reshape(bp, 8, D_FF)[:, 8 - (FFN_CONV_WIDTH - 1):]
    fs_s = a_tail[bp * 8:].reshape(bs, 8, D_FF)[:, 8 - (FFN_CONV_WIDTH - 1):]

    st = lambda a: a[None]
    return (yp, ys, st(ckv_p), st(kr_p), st(cs_p), st(fs_p),
            st(mk_p.reshape(bp, MEM_LEN, MEM_HEADS, MEM_HEAD_DIM)),
            st(mv_p.reshape(bp, MEM_LEN, MEM_HEADS, MEM_HEAD_DIM)),
            st(ckv_s), st(kr_s), st(cs_s), st(fs_s))
```

```python
import functools

import numpy as np
import jax
import jax.numpy as jnp
from jax import lax
from jax.experimental import pallas as pl
from jax.experimental.pallas import tpu as pltpu

D_MODEL = 2048
CHUNK = 64
EPS = 1e-6
N_HEADS = 16
QK_NOPE = 128
QK_ROPE = 64
V_HEAD = 128
Q_LORA = 512
KV_LORA = 512
ROPE_THETA = 10000.0
MLA_SCALE = (QK_NOPE + QK_ROPE) ** -0.5
CONV_DIM = 1024
CONV_WIDTH = 31
MEM_LEN = 256
MEM_HEADS = 4
MEM_HEAD_DIM = 256
MEM_DIM = MEM_HEADS * MEM_HEAD_DIM
MEM_SCALE = MEM_HEAD_DIM ** -0.5
N_BRANCH = 3
D_FF = 5632
FFN_CONV_WIDTH = 3
OFF_CKV = Q_LORA
OFF_KR = OFF_CKV + KV_LORA
OFF_GLU = OFF_KR + QK_ROPE
OFF_QM = OFF_GLU + 2 * CONV_DIM
OFF_GATE = OFF_QM + MEM_DIM

LANES = 128
V7X_VMEM_LIMIT = 56 * 1024 * 1024
Q_HEAD_W = 2 * LANES
V_HEAD_W = 2 * LANES
NEG_BIG = -1e30
Q_SCALE = MLA_SCALE * float(np.log2(np.e))

BF16 = jnp.bfloat16
F32 = jnp.float32


def _cparams(sem, vmem=V7X_VMEM_LIMIT):
    return pltpu.CompilerParams(dimension_semantics=sem, vmem_limit_bytes=vmem)


def _row_tile(n, cap, mult=16):
    t = min(cap, n)
    t -= t % mult
    while t > mult and n % t:
        t -= mult
    assert t >= mult and n % t == 0, (n, cap)
    return t


def _rms(x, g):
    y = x * lax.rsqrt(jnp.mean(x * x, axis=-1, keepdims=True) + EPS)
    return y * g


def _dot(a, b):
    return jnp.dot(a, b, preferred_element_type=F32)


def _dot_nt(a, b):
    return lax.dot_general(a, b, (((1,), (1,)), ((), ())), preferred_element_type=F32)


def _proj_lat_kernel(x_ref, gmix_ref, w_ref, gcq_ref, gckv_ref, cos_ref, sin_ref,
                     xn_ref, cqn_ref, ckv_ref, ckvb_ref, kr_ref, krpad_ref):
    xn = _rms(x_ref[...], gmix_ref[...]).astype(BF16)
    xn_ref[...] = xn
    z = _dot(xn, w_ref[...])
    cqn_ref[...] = _rms(z[:, :Q_LORA], gcq_ref[...]).astype(BF16)
    ckv = _rms(z[:, OFF_CKV:OFF_KR], gckv_ref[...])
    ckv_ref[...] = ckv
    ckvb_ref[...] = ckv.astype(BF16)
    r = z[:, OFF_KR:OFF_KR + LANES] * cos_ref[...] + z[:, OFF_KR + LANES:] * sin_ref[...]
    kr_ref[...] = r[:, :QK_ROPE]
    lane = lax.broadcasted_iota(jnp.int32, r.shape, 1)
    zero = jnp.zeros_like(r)
    krpad_ref[:, :LANES] = jnp.where(lane < QK_ROPE, r, zero).astype(BF16)
    krpad_ref[:, LANES:] = jnp.where(lane >= QK_ROPE, r, zero).astype(BF16)


def _proj_lat(x, g_mix, w1, g_cq, g_ckv, cos_t, sin_t, tm):
    n = x.shape[0]
    nt = cos_t.shape[0] // tm
    row = lambda i: (i, 0)
    const = lambda i: (0, 0)
    tab = lambda i: (i % nt, 0)
    return pl.pallas_call(
        _proj_lat_kernel,
        grid=(n // tm,),
        in_specs=[
            pl.BlockSpec((tm, D_MODEL), row),
            pl.BlockSpec((1, D_MODEL), const),
            pl.BlockSpec(w1.shape, const),
            pl.BlockSpec((1, Q_LORA), const),
            pl.BlockSpec((1, KV_LORA), const),
            pl.BlockSpec((tm, LANES), tab),
            pl.BlockSpec((tm, LANES), tab),
        ],
        out_specs=[
            pl.BlockSpec((tm, D_MODEL), row),
            pl.BlockSpec((tm, Q_LORA), row),
            pl.BlockSpec((tm, KV_LORA), row),
            pl.BlockSpec((tm, KV_LORA), row),
            pl.BlockSpec((tm, QK_ROPE), row),
            pl.BlockSpec((tm, Q_HEAD_W), row),
        ],
        out_shape=[
            jax.ShapeDtypeStruct((n, D_MODEL), BF16),
            jax.ShapeDtypeStruct((n, Q_LORA), BF16),
            jax.ShapeDtypeStruct((n, KV_LORA), F32),
            jax.ShapeDtypeStruct((n, KV_LORA), BF16),
            jax.ShapeDtypeStruct((n, QK_ROPE), F32),
            jax.ShapeDtypeStruct((n, Q_HEAD_W), BF16),
        ],
        compiler_params=_cparams(("parallel",)),
        name="proj_lat",
    )(x, g_mix, w1, g_cq, g_ckv, cos_t, sin_t)


def _proj_q_kernel(cqn_ref, w_ref, cos_ref, sin_ref, q_ref):
    cqn = cqn_ref[...]
    cos = cos_ref[...]
    sin = sin_ref[...]
    pw = 4 * LANES
    for p in range(N_HEADS // 2):
        r = _dot(cqn, w_ref[:, p * pw:(p + 1) * pw])
        rp = ((r[:, 2 * LANES:3 * LANES] * cos + r[:, 3 * LANES:] * sin) * Q_SCALE).astype(BF16)
        q_ref[:, p * pw:p * pw + LANES] = (r[:, :LANES] * Q_SCALE).astype(BF16)
        q_ref[:, p * pw + LANES:p * pw + 2 * LANES] = rp
        q_ref[:, p * pw + 2 * LANES:p * pw + 3 * LANES] = (r[:, LANES:2 * LANES] * Q_SCALE).astype(BF16)
        q_ref[:, p * pw + 3 * LANES:(p + 1) * pw] = rp


def _proj_q(cqn, w2, cos_t, sin_t, tm):
    n = cqn.shape[0]
    nt = cos_t.shape[0] // tm
    return pl.pallas_call(
        _proj_q_kernel,
        grid=(n // tm,),
        in_specs=[
            pl.BlockSpec((tm, Q_LORA), lambda i: (i, 0)),
            pl.BlockSpec(w2.shape, lambda i: (0, 0)),
            pl.BlockSpec((tm, LANES), lambda i: (i % nt, 0)),
            pl.BlockSpec((tm, LANES), lambda i: (i % nt, 0)),
        ],
        out_specs=pl.BlockSpec((tm, N_HEADS * Q_HEAD_W), lambda i: (i, 0)),
        out_shape=jax.ShapeDtypeStruct((n, N_HEADS * Q_HEAD_W), BF16),
        compiler_params=_cparams(("parallel",)),
        name="proj_q",
    )(cqn, w2, cos_t, sin_t)


def _proj_kv_kernel(c_ref, kr_ref, w_ref, k_ref, v_ref):
    c = c_ref[...]
    hw = N_HEADS * QK_NOPE
    kn = _dot(c, w_ref[:, :hw]).astype(BF16)
    vv = _dot(c, w_ref[:, hw:]).astype(BF16)
    ones = jnp.ones((c.shape[0], LANES), BF16)
    for h in range(N_HEADS):
        par = h % 2
        k_ref[:, h * Q_HEAD_W:h * Q_HEAD_W + QK_NOPE] = kn[:, h * QK_NOPE:(h + 1) * QK_NOPE]
        k_ref[:, h * Q_HEAD_W + QK_NOPE:(h + 1) * Q_HEAD_W] = kr_ref[:, par * LANES:(par + 1) * LANES]
        v_ref[:, h * V_HEAD_W:h * V_HEAD_W + V_HEAD] = vv[:, h * V_HEAD:(h + 1) * V_HEAD]
        v_ref[:, h * V_HEAD_W + V_HEAD:(h + 1) * V_HEAD_W] = ones


def _proj_kv(ckv_b, krpad, w3, tm):
    n = ckv_b.shape[0]
    return pl.pallas_call(
        _proj_kv_kernel,
        grid=(n // tm,),
        in_specs=[
            pl.BlockSpec((tm, KV_LORA), lambda i: (i, 0)),
            pl.BlockSpec((tm, Q_HEAD_W), lambda i: (i, 0)),
            pl.BlockSpec(w3.shape, lambda i: (0, 0)),
        ],
        out_specs=[pl.BlockSpec((tm, N_HEADS * Q_HEAD_W), lambda i: (i, 0)),
                   pl.BlockSpec((tm, N_HEADS * V_HEAD_W), lambda i: (i, 0))],
        out_shape=[jax.ShapeDtypeStruct((n, N_HEADS * Q_HEAD_W), BF16),
                   jax.ShapeDtypeStruct((n, N_HEADS * V_HEAD_W), BF16)],
        compiler_params=_cparams(("parallel",)),
        name="proj_kv",
    )(ckv_b, krpad, w3)


def _attn_schedule(t, s, p, tq, tk):
    qi_l, ki_l, fl_l = [], [], []
    for qi in range(t // tq):
        q_lo = p + qi * tq
        q_hi = p + (qi + 1) * tq - 1
        kmax = min(s, (q_hi // CHUNK + 1) * CHUNK)
        full_vis = (q_lo // CHUNK + 1) * CHUNK
        nk = -(-kmax // tk)
        for ki in range(nk):
            need_mask = (ki + 1) * tk > full_vis
            qi_l.append(qi)
            ki_l.append(ki)
            fl_l.append(int(ki == 0) | (int(ki == nk - 1) << 1) | (int(need_mask) << 2))
    return (np.asarray(qi_l, np.int32), np.asarray(ki_l, np.int32), np.asarray(fl_l, np.int32))


def _attn_kernel(qi_tab, ki_tab, fl_tab, q_ref, k_ref, v_ref, o_ref, m_scr, acc_scr,
                 *, hg, tq, tk, past):
    step = pl.program_id(2)
    flags = fl_tab[step]
    qi = qi_tab[step]
    ki = ki_tab[step]

    @pl.when((flags & 1) != 0)
    def _():
        m_scr[...] = jnp.full(m_scr.shape, NEG_BIG, F32)
        acc_scr[...] = jnp.zeros(acc_scr.shape, F32)

    def body(masked):
        if masked:
            qpos = past + qi * tq + lax.broadcasted_iota(jnp.int32, (tq, tk), 0)
            kpos = ki * tk + lax.broadcasted_iota(jnp.int32, (tq, tk), 1)
            visible = (kpos // CHUNK) <= (qpos // CHUNK)
        for h in range(hg):
            q = q_ref[0, :, h * Q_HEAD_W:(h + 1) * Q_HEAD_W]
            k = k_ref[0, :, h * Q_HEAD_W:(h + 1) * Q_HEAD_W]
            sc = _dot_nt(q, k)
            if masked:
                sc = jnp.where(visible, sc, NEG_BIG)
            m_prev = m_scr[h]
            m_cur = jnp.max(sc, axis=1, keepdims=True)
            m_next = jnp.maximum(m_prev, m_cur)
            if tk % LANES == 0:
                p = jnp.exp2(sc - jnp.concatenate([m_next] * (tk // LANES), axis=1))
            else:
                p = jnp.exp2(sc - m_next[:, :1])
            alpha = jnp.exp2(m_prev - m_next)
            m_scr[h] = m_next
            pv = _dot(p.astype(BF16), v_ref[0, :, h * V_HEAD_W:(h + 1) * V_HEAD_W])
            acc_scr[h] = acc_scr[h] * jnp.concatenate([alpha] * (V_HEAD_W // LANES), axis=1) + pv

    @pl.when((flags & 4) != 0)
    def _():
        body(True)

    @pl.when((flags & 4) == 0)
    def _():
        body(False)

    @pl.when((flags & 2) != 0)
    def _():
        for h in range(hg):
            acc = acc_scr[h]
            o_ref[0, :, h * V_HEAD:(h + 1) * V_HEAD] = (acc[:, :V_HEAD] / acc[:, V_HEAD:]).astype(o_ref.dtype)


def _attention(q, k, v, past, tq, tk, hg):
    b, t, _ = q.shape
    s = k.shape[1]
    qi_np, ki_np, fl_np = _attn_schedule(t, s, past, tq, tk)
    n_steps = int(qi_np.shape[0])
    kernel = functools.partial(_attn_kernel, hg=hg, tq=tq, tk=tk, past=past)
    grid_spec = pltpu.PrefetchScalarGridSpec(
        num_scalar_prefetch=3,
        grid=(b, N_HEADS // hg, n_steps),
        in_specs=[
            pl.BlockSpec((1, tq, hg * Q_HEAD_W), lambda bi, g, st, qt, kt, ft: (bi, qt[st], g)),
            pl.BlockSpec((1, tk, hg * Q_HEAD_W), lambda bi, g, st, qt, kt, ft: (bi, kt[st], g)),
            pl.BlockSpec((1, tk, hg * V_HEAD_W), lambda bi, g, st, qt, kt, ft: (bi, kt[st], g)),
        ],
        out_specs=pl.BlockSpec((1, tq, hg * V_HEAD), lambda bi, g, st, qt, kt, ft: (bi, qt[st], g)),
        scratch_shapes=[
            pltpu.VMEM((hg, tq, LANES), F32),
            pltpu.VMEM((hg, tq, V_HEAD_W), F32),
        ],
    )
    return pl.pallas_call(
        kernel,
        grid_spec=grid_spec,
        out_shape=jax.ShapeDtypeStruct((b, t, N_HEADS * V_HEAD), BF16),
        compiler_params=_cparams(("parallel", "parallel", "arbitrary")),
        name="mla_attention",
    )(jnp.asarray(qi_np), jnp.asarray(ki_np), jnp.asarray(fl_np), q, k, v)


CONV_HALO = 32
SUBLANES = 8
CONV_ROWS = 32
GLU_CHAINS = 1


def _glu_conv_kernel(xn_ref, w_ref, hist_ref, wdw_ref, b_ref, g_ref, beta_ref, mk_ref, mv_ref,
                     om_ref, cv_ref, tail_ref, carry, wb, *chain_scratch, tm, t_len):
    i = pl.program_id(0)
    gbufs, shifteds, cbufs = chain_scratch[0::3], chain_scratch[1::3], chain_scratch[2::3]
    chains = len(gbufs)
    rows = tm // chains

    @pl.when((i * tm) % t_len == 0)
    def _():
        carry[...] = hist_ref[0]

    for k in range(CONV_WIDTH):
        wb[k] = jnp.broadcast_to(wdw_ref[k:k + 1, :], (SUBLANES, CONV_DIM))
    wb[CONV_WIDTH] = jnp.broadcast_to(b_ref[...], (SUBLANES, CONV_DIM))

    off = CONV_HALO - (CONV_WIDTH - 1)
    groups = CONV_ROWS // SUBLANES
    for ch in range(chains):
        r0 = ch * rows
        gbuf, shifted, cbuf = gbufs[ch], shifteds[ch], cbufs[ch]
        gbuf[0:CONV_HALO, :] = carry[...] if ch == 0 else gbufs[ch - 1][rows:rows + CONV_HALO, :]
        xn = xn_ref[r0:r0 + rows, :]
        a = _dot(xn, w_ref[:, :CONV_DIM])
        bgate = _dot(xn, w_ref[:, CONV_DIM:2 * CONV_DIM])
        gbuf[CONV_HALO:, :] = a * jax.nn.sigmoid(bgate)
        qm = _dot(xn, w_ref[:, 2 * CONV_DIM:]).astype(BF16)
        for h in range(MEM_HEADS):
            sl = slice(h * MEM_HEAD_DIM, (h + 1) * MEM_HEAD_DIM)
            sc = _dot_nt(qm[:, sl], mk_ref[0, :, sl]) * MEM_SCALE
            e = jnp.exp(sc - jnp.max(sc, axis=-1, keepdims=True))
            o = _dot(e.astype(BF16), mv_ref[0, :, sl])
            om_ref[r0:r0 + rows, sl] = (o / jnp.sum(e, axis=-1, keepdims=True)).astype(om_ref.dtype)
        span = rows + CONV_HALO - SUBLANES
        for p in range(1, SUBLANES):
            shifted[p - 1] = gbuf[p:p + span, :]
        for c0 in range(0, rows, CONV_ROWS):
            accs = [wb[CONV_WIDTH]] * groups
            for k in range(CONV_WIDTH):
                s = off + k + c0
                p = s % SUBLANES
                wk = wb[k]
                for g in range(groups):
                    if p == 0:
                        rws = gbuf[s + g * SUBLANES:s + (g + 1) * SUBLANES, :]
                    else:
                        rws = shifted[p - 1, s - p + g * SUBLANES:s - p + (g + 1) * SUBLANES, :]
                    accs[g] = accs[g] + wk * rws
            for g in range(groups):
                cbuf[c0 + g * SUBLANES:c0 + (g + 1) * SUBLANES, :] = accs[g]
        cv = cbuf[...]
        mu = jnp.mean(cv, axis=-1, keepdims=True)
        d = cv - mu
        var = jnp.mean(d * d, axis=-1, keepdims=True)
        y = d * lax.rsqrt(var + EPS) * g_ref[...] + beta_ref[...]
        cv_ref[r0:r0 + rows, :] = (y * jax.nn.sigmoid(y)).astype(cv_ref.dtype)

    last = gbufs[-1][rows:rows + CONV_HALO, :]
    tail_ref[0] = last
    carry[...] = last


def _glu_conv(xn, w5, hist, w_dw, b_dw, ln_g, ln_b, mem_k, mem_v, t_len, tm):
    n = xn.shape[0]
    assert t_len % tm == 0 and tm % CONV_ROWS == 0 and tm >= CONV_HALO
    nt = t_len // tm
    chains = GLU_CHAINS if tm % (GLU_CHAINS * CONV_ROWS) == 0 else 1
    rows = tm // chains
    chain_scratch = [pltpu.VMEM((rows + CONV_HALO, CONV_DIM), F32),
                     pltpu.VMEM((SUBLANES - 1, rows + CONV_HALO - SUBLANES, CONV_DIM), F32),
                     pltpu.VMEM((rows, CONV_DIM), F32)] * chains
    kernel = functools.partial(_glu_conv_kernel, tm=tm, t_len=t_len)
    const = lambda i: (0, 0)
    return pl.pallas_call(
        kernel,
        grid=(n // tm,),
        in_specs=[
            pl.BlockSpec((tm, D_MODEL), lambda i: (i, 0)),
            pl.BlockSpec(w5.shape, const, pipeline_mode=pl.Buffered(1)),
            pl.BlockSpec((1, CONV_HALO, CONV_DIM), lambda i: (i // nt, 0, 0)),
            pl.BlockSpec(w_dw.shape, const),
            pl.BlockSpec((1, CONV_DIM), const),
            pl.BlockSpec((1, CONV_DIM), const),
            pl.BlockSpec((1, CONV_DIM), const),
            pl.BlockSpec((1, MEM_LEN, MEM_DIM), lambda i: (i // nt, 0, 0)),
            pl.BlockSpec((1, MEM_LEN, MEM_DIM), lambda i: (i // nt, 0, 0)),
        ],
        out_specs=[pl.BlockSpec((tm, MEM_DIM), lambda i: (i, 0)),
                   pl.BlockSpec((tm, CONV_DIM), lambda i: (i, 0)),
                   pl.BlockSpec((1, CONV_HALO, CONV_DIM), lambda i: (i // nt, 0, 0))],
        out_shape=[jax.ShapeDtypeStruct((n, MEM_DIM), BF16),
                   jax.ShapeDtypeStruct((n, CONV_DIM), BF16),
                   jax.ShapeDtypeStruct((n // t_len, CONV_HALO, CONV_DIM), F32)],
        scratch_shapes=[pltpu.VMEM((CONV_HALO, CONV_DIM), F32),
                        pltpu.VMEM((CONV_WIDTH + 1, SUBLANES, CONV_DIM), F32)] + chain_scratch,
        compiler_params=_cparams(("arbitrary",)),
        name="glu_conv",
    )(xn, w5, hist, w_dw, b_dw, ln_g, ln_b, mem_k, mem_v)


def _norm_matmul_kernel(x_ref, g_ref, w_ref, o_ref):
    xn = _rms(x_ref[...], g_ref[...]).astype(BF16)
    o_ref[...] = _dot(xn, w_ref[...])


def _norm_matmul(x, g, w, tm, tn, nout=None):
    n, kdim = x.shape
    nout = w.shape[1] if nout is None else nout
    return pl.pallas_call(
        _norm_matmul_kernel,
        grid=(n // tm, nout // tn),
        in_specs=[
            pl.BlockSpec((tm, kdim), lambda i, j: (i, 0)),
            pl.BlockSpec((1, kdim), lambda i, j: (0, 0)),
            pl.BlockSpec((kdim, tn), lambda i, j: (0, j)),
        ],
        out_specs=pl.BlockSpec((tm, tn), lambda i, j: (i, j)),
        out_shape=jax.ShapeDtypeStruct((n, nout), F32),
        compiler_params=_cparams(("parallel", "arbitrary")),
        name="norm_matmul",
    )(x, g, w)


MIX_CHAINS = 4


def _mix_kernel(attn_ref, cv_ref, om_ref, xn_ref, wa_ref, wb_ref, wc_ref,
                wg0_ref, wg1_ref, wg2_ref, bg0_ref, bg1_ref, bg2_ref, mix_ref):
    tm = mix_ref.shape[0]
    rows = tm // MIX_CHAINS
    for ch in range(MIX_CHAINS):
        sl = slice(ch * rows, (ch + 1) * rows)
        xn = xn_ref[sl, :]
        a = _dot(attn_ref[sl, :], wa_ref[...])
        bo = _dot(cv_ref[sl, :], wb_ref[...])
        c = _dot(om_ref[sl, :], wc_ref[...])
        g0 = jax.nn.sigmoid(_dot(xn, wg0_ref[...]) + bg0_ref[...])
        g1 = jax.nn.sigmoid(_dot(xn, wg1_ref[...]) + bg1_ref[...])
        g2 = jax.nn.sigmoid(_dot(xn, wg2_ref[...]) + bg2_ref[...])
        mix_ref[sl, :] = (g0 * a + g1 * bo + g2 * c).astype(BF16)


def _branch_mix(attn, cv, om, xn, w_mla_o, w_conv_o, w_mem_o, wg, b_gate, tm, tn):
    n = xn.shape[0]
    nj = D_MODEL // tn
    assert wg.shape[1] == N_BRANCH * D_MODEL and b_gate.shape[1] == N_BRANCH * D_MODEL
    row = lambda j, i: (i, 0)
    col = lambda j, i: (0, j)
    return pl.pallas_call(
        _mix_kernel,
        grid=(nj, n // tm),
        in_specs=[
            pl.BlockSpec((tm, N_HEADS * V_HEAD), row),
            pl.BlockSpec((tm, CONV_DIM), row),
            pl.BlockSpec((tm, MEM_DIM), row),
            pl.BlockSpec((tm, D_MODEL), row),
            pl.BlockSpec((N_HEADS * V_HEAD, tn), col),
            pl.BlockSpec((CONV_DIM, tn), col),
            pl.BlockSpec((MEM_DIM, tn), col),
            pl.BlockSpec((D_MODEL, tn), lambda j, i: (0, j)),
            pl.BlockSpec((D_MODEL, tn), lambda j, i: (0, nj + j)),
            pl.BlockSpec((D_MODEL, tn), lambda j, i: (0, 2 * nj + j)),
            pl.BlockSpec((1, tn), lambda j, i: (0, j)),
            pl.BlockSpec((1, tn), lambda j, i: (0, nj + j)),
            pl.BlockSpec((1, tn), lambda j, i: (0, 2 * nj + j)),
        ],
        out_specs=pl.BlockSpec((tm, tn), lambda j, i: (i, j)),
        out_shape=jax.ShapeDtypeStruct((n, D_MODEL), BF16),
        compiler_params=_cparams(("parallel", "parallel")),
        name="branch_mix",
    )(attn, cv, om, xn, w_mla_o, w_conv_o, w_mem_o, wg, wg, wg, b_gate, b_gate, b_gate)


def _out_proj_kernel(mix_ref, wo_ref, x_ref, gffn_ref, h_ref, hn_ref):
    h = x_ref[...] + _dot(mix_ref[...], wo_ref[...])
    h_ref[...] = h
    hn_ref[...] = _rms(h, gffn_ref[...]).astype(BF16)


def _out_proj(mix, w_out, x, g_ffn, tm):
    n = x.shape[0]
    row = lambda i: (i, 0)
    return pl.pallas_call(
        _out_proj_kernel,
        grid=(n // tm,),
        in_specs=[
            pl.BlockSpec((tm, D_MODEL), row),
            pl.BlockSpec((D_MODEL, D_MODEL), lambda i: (0, 0), pipeline_mode=pl.Buffered(1)),
            pl.BlockSpec((tm, D_MODEL), row),
            pl.BlockSpec((1, D_MODEL), lambda i: (0, 0)),
        ],
        out_specs=[pl.BlockSpec((tm, D_MODEL), row), pl.BlockSpec((tm, D_MODEL), row)],
        out_shape=[jax.ShapeDtypeStruct((n, D_MODEL), F32),
                   jax.ShapeDtypeStruct((n, D_MODEL), BF16)],
        compiler_params=_cparams(("parallel",)),
        name="out_proj",
    )(mix, w_out, x, g_ffn)


FFN_HALO = 16
FFN_CHAINS = 2


def _ffn_kernel(hn_ref, halo_ref, hist_ref, wa_ref, wv_ref, wdw_ref, bdw_ref, wd_ref, h_ref,
                gfin_ref, y_ref, abuf, *, tm, t_len):
    i = pl.program_id(0)
    j = pl.program_id(1)

    @pl.when(j == 0)
    def _():
        y_ref[...] = h_ref[...]

    def conv3(a, b0, rows):
        return (wdw_ref[2:3, :] * a
                + wdw_ref[1:2, :] * abuf[b0 - 1:b0 - 1 + rows, :]
                + wdw_ref[0:1, :] * abuf[b0 - 2:b0 - 2 + rows, :]
                + bdw_ref[...])

    nseq = hist_ref.shape[0]
    if nseq == 1:
        a_prev = _dot(halo_ref[...], wa_ref[...])
        at_seq_start = (i * tm) % t_len == 0
        abuf[0:FFN_HALO, :] = jnp.where(at_seq_start, hist_ref[0], a_prev)
        chains = FFN_CHAINS if tm % (FFN_CHAINS * FFN_HALO) == 0 else 1
        rows = tm // chains
        for ch in range(chains):
            r0 = ch * rows
            hn = hn_ref[r0:r0 + rows, :]
            a = _dot(hn, wa_ref[...])
            val = _dot(hn, wv_ref[...])
            abuf[FFN_HALO + r0:FFN_HALO + r0 + rows, :] = a
            conv = conv3(a, FFN_HALO + r0, rows)
            act = (conv * jax.nn.sigmoid(conv) * val).astype(BF16)
            y_ref[r0:r0 + rows, :] += _dot(act, wd_ref[...])
    else:
        seg = tm // nseq
        hn = hn_ref[...]
        a = _dot(hn, wa_ref[...])
        val = _dot(hn, wv_ref[...])
        convs = []
        for s in range(nseq):
            base = s * (seg + FFN_HALO)
            a_s = a[s * seg:(s + 1) * seg, :]
            abuf[base:base + FFN_HALO, :] = hist_ref[s]
            abuf[base + FFN_HALO:base + FFN_HALO + seg, :] = a_s
            convs.append(conv3(a_s, base + FFN_HALO, seg))
        conv = jnp.concatenate(convs, axis=0)
        act = (conv * jax.nn.sigmoid(conv) * val).astype(BF16)
        y_ref[...] += _dot(act, wd_ref[...])

    @pl.when(j == pl.num_programs(1) - 1)
    def _():
        y_ref[...] = _rms(y_ref[...], gfin_ref[...])


def _ffn(hn, hist, w_up, w_dw, b_dw, w_down, h, g_final, t_len, tm, tn):
    n = hn.shape[0]
    nj = D_FF // tn
    r = tm // FFN_HALO
    nseq = max(1, tm // t_len)
    assert tm % t_len == 0 or t_len % tm == 0
    kernel = functools.partial(_ffn_kernel, tm=tm, t_len=t_len)
    return pl.pallas_call(
        kernel,
        grid=(n // tm, nj),
        in_specs=[
            pl.BlockSpec((tm, D_MODEL), lambda i, j: (i, 0)),
            pl.BlockSpec((FFN_HALO, D_MODEL), lambda i, j: (jnp.maximum(i * r - 1, 0), 0)),
            pl.BlockSpec((nseq, FFN_HALO, tn), lambda i, j: ((i * tm) // (t_len * nseq), 0, j)),
            pl.BlockSpec((D_MODEL, tn), lambda i, j: (0, j)),
            pl.BlockSpec((D_MODEL, tn), lambda i, j: (0, nj + j)),
            pl.BlockSpec((FFN_CONV_WIDTH, tn), lambda i, j: (0, j)),
            pl.BlockSpec((1, tn), lambda i, j: (0, j)),
            pl.BlockSpec((tn, D_MODEL), lambda i, j: (j, 0)),
            pl.BlockSpec((tm, D_MODEL), lambda i, j: (i, 0), pipeline_mode=pl.Buffered(1)),
            pl.BlockSpec((1, D_MODEL), lambda i, j: (0, 0)),
        ],
        out_specs=pl.BlockSpec((tm, D_MODEL), lambda i, j: (i, 0)),
        out_shape=jax.ShapeDtypeStruct((n, D_MODEL), F32),
        scratch_shapes=[pltpu.VMEM((tm + nseq * FFN_HALO, tn), F32)],
        compiler_params=_cparams(("parallel", "arbitrary")),
        name="conv_ffn",
    )(hn, hn, hist, w_up, w_up, w_dw, b_dw, w_down, h, g_final)


def _rope_tables(pos):
    half = QK_ROPE // 2
    inv_freq = ROPE_THETA ** (-jnp.arange(half, dtype=F32) / half)
    ang = pos.astype(F32)[:, None] * inv_freq[None, :]
    cos = jnp.cos(ang)
    sin = jnp.sin(ang)
    cos_t = jnp.tile(cos, (1, LANES // half))
    sin_t = jnp.tile(jnp.concatenate([-sin, sin], axis=1), (1, LANES // QK_ROPE))
    return cos_t, sin_t


def _swap_halves(w):
    half = QK_ROPE // 2
    return jnp.concatenate([w[..., half:], w[..., :half]], axis=-1)


def _prep_weights(w_in, w_uq, w_ukv, w_mla_o, w_conv_o, w_mem_o, w_out, w_up, w_down,
                  w_mem_k, w_mem_v, w_conv_dw):
    kr = w_in[:, OFF_KR:OFF_GLU]
    kr_sw = _swap_halves(kr)
    w1 = jnp.concatenate([w_in[:, :OFF_KR], kr, kr, kr_sw, kr_sw], axis=1).astype(BF16)
    w5 = w_in[:, OFF_GLU:OFF_GATE].astype(BF16)
    wg = w_in[:, OFF_GATE:].astype(BF16)
    nope = w_uq[:, :, :QK_NOPE].reshape(Q_LORA, N_HEADS // 2, 2 * QK_NOPE)
    rope = w_uq[:, :, QK_NOPE:]
    rope_p = rope.reshape(Q_LORA, N_HEADS // 2, 2 * QK_ROPE)
    rope_sw = _swap_halves(rope).reshape(Q_LORA, N_HEADS // 2, 2 * QK_ROPE)
    w2 = jnp.concatenate([nope, rope_p, rope_sw], axis=2).reshape(Q_LORA, -1).astype(BF16)
    w3 = jnp.concatenate([w_ukv[:, :, :QK_NOPE].reshape(KV_LORA, -1),
                          w_ukv[:, :, QK_NOPE:].reshape(KV_LORA, -1)], axis=1).astype(BF16)
    wmem = jnp.concatenate([w_mem_k, w_mem_v], axis=1).astype(BF16)
    wdw = jnp.concatenate([w_conv_dw, jnp.zeros((1, CONV_DIM), F32)], axis=0)
    return dict(w1=w1, w5=w5, wg=wg, w2=w2, w3=w3, wmem=wmem, wdw=wdw,
                w_mla_o=w_mla_o.astype(BF16), w_conv_o=w_conv_o.astype(BF16),
                w_mem_o=w_mem_o.astype(BF16), w_out=w_out.astype(BF16),
                w_up=w_up.astype(BF16), w_down=w_down.astype(BF16))


def _pad_hist(hist, rows):
    b, k, c = hist.shape
    return jnp.concatenate([jnp.zeros((b, rows - k, c), hist.dtype), hist], axis=1)


def _encoder_layer(x, past_len, hist_conv, hist_ffn, past_ckv, past_krope, mem_k, mem_v, wts, prm,
                   cfg):
    b, t, _ = x.shape
    n = b * t
    pos = past_len + jnp.arange(t)
    cos_t, sin_t = _rope_tables(pos)
    tm = cfg["tm"]
    if tm > t:
        cos_t = jnp.tile(cos_t, (tm // t, 1))
        sin_t = jnp.tile(sin_t, (tm // t, 1))
    x2 = x.reshape(n, D_MODEL)

    xn, cqn, ckv, ckv_b, krope, krpad = _proj_lat(
        x2, prm["g_mix"], wts["w1"], prm["g_cq"], prm["g_ckv"], cos_t, sin_t, tm)
    if past_len:
        pk = past_krope.astype(BF16)
        zeros = jnp.zeros_like(pk)
        past_pad = jnp.concatenate([pk, zeros, zeros, pk], axis=-1)
        ckv_all = jnp.concatenate([past_ckv.astype(BF16), ckv_b.reshape(b, t, KV_LORA)], axis=1)
        kr_all = jnp.concatenate([past_pad, krpad.reshape(b, t, Q_HEAD_W)], axis=1)
    else:
        ckv_all = ckv_b.reshape(b, t, KV_LORA)
        kr_all = krpad.reshape(b, t, Q_HEAD_W)
    s = past_len + t
    q = _proj_q(cqn, wts["w2"], cos_t, sin_t, tm)
    kf, vf = _proj_kv(ckv_all.reshape(b * s, KV_LORA), kr_all.reshape(b * s, Q_HEAD_W), wts["w3"],
                      _row_tile(b * s, cfg["tm_kv"]))
    attn = _attention(q.reshape(b, t, -1), kf.reshape(b, s, -1), vf.reshape(b, s, -1),
                      past_len, cfg["tq"], cfg["tk"], cfg["hg"])

    om, cv, glu_tail = _glu_conv(xn, wts["w5"], _pad_hist(hist_conv, CONV_HALO), wts["wdw"],
                                 prm["b_conv_dw"], prm["ln_conv_g"], prm["ln_conv_b"], mem_k, mem_v,
                                 t, cfg["tm_glu"])

    mix = _branch_mix(attn.reshape(n, -1), cv.reshape(n, CONV_DIM), om.reshape(n, MEM_DIM), xn,
                      wts["w_mla_o"], wts["w_conv_o"], wts["w_mem_o"], wts["wg"], prm["b_gate"],
                      cfg["tm_mix"], cfg["tn_mix"])
    h, hn = _out_proj(mix, wts["w_out"], x2, prm["g_ffn"], tm)
    y = _ffn(hn, _pad_hist(hist_ffn, FFN_HALO), wts["w_up"], prm["w_ffn_dw"], prm["b_ffn_dw"],
             wts["w_down"], h, prm["g_final"], t, cfg["tm_ffn"], cfg["tn_ffn"])

    new_hist_conv = glu_tail[:, CONV_HALO - (CONV_WIDTH - 1):]
    h_tail = h.reshape(b, t, D_MODEL)[:, t - 8:]
    return (y.reshape(b, t, D_MODEL), ckv.reshape(b, t, KV_LORA), krope.reshape(b, t, QK_ROPE),
            new_hist_conv, h_tail)


def kernel(x_prompt, x_sample, mem_prompt, cache_ckv, cache_krope, state_conv, state_ffn_conv,
           cache_mem_k, cache_mem_v, g_mix, w_in, g_cq, w_uq, g_ckv, w_ukv, w_mla_o,
           w_conv_dw, b_conv_dw, ln_conv_g, ln_conv_b, w_conv_o, g_mem, w_mem_k, w_mem_v,
           w_mem_o, b_gate, w_out, g_ffn, w_up, w_ffn_dw, b_ffn_dw, w_down, g_final):
    depth = g_mix.shape[0]
    assert depth == 1
    l = 0
    bp, tp, _ = x_prompt.shape
    bs, ts, _ = x_sample.shape
    past = cache_ckv.shape[2]
    assert tp >= CONV_HALO and ts >= CONV_HALO

    wts = _prep_weights(w_in[l], w_uq[l], w_ukv[l], w_mla_o[l], w_conv_o[l], w_mem_o[l], w_out[l],
                        w_up[l], w_down[l], w_mem_k[l], w_mem_v[l], w_conv_dw[l])
    row = lambda a: a.reshape(1, -1)
    prm = dict(g_mix=row(g_mix[l]), g_cq=row(g_cq[l]), g_ckv=row(g_ckv[l]),
               b_conv_dw=row(b_conv_dw[l]), ln_conv_g=row(ln_conv_g[l]), ln_conv_b=row(ln_conv_b[l]),
               b_gate=row(b_gate[l]), g_ffn=row(g_ffn[l]), w_ffn_dw=w_ffn_dw[l],
               b_ffn_dw=row(b_ffn_dw[l]), g_final=row(g_final))

    mem2 = mem_prompt.reshape(bp * MEM_LEN, D_MODEL)
    mkv = _norm_matmul(mem2, row(g_mem[l]), wts["wmem"], 256, 512)
    mk_p = mkv[:, :MEM_DIM].reshape(bp, MEM_LEN, MEM_DIM)
    mv_p = mkv[:, MEM_DIM:].reshape(bp, MEM_LEN, MEM_DIM)

    cfg_p = dict(tm=512, tm_kv=512, tq=512, tk=512, hg=16, tm_glu=512,
                 tm_mix=1024, tn_mix=512, tm_ffn=1024, tn_ffn=512)
    n_s = bs * ts
    cfg_s = dict(tm=n_s, tm_kv=512, tq=ts, tk=past + ts, hg=8, tm_glu=ts,
                 tm_mix=n_s, tn_mix=256, tm_ffn=n_s, tn_ffn=512)

    zeros_conv = jnp.zeros((bp, CONV_WIDTH - 1, CONV_DIM), F32)
    zeros_ffn = jnp.zeros((bp, FFN_CONV_WIDTH - 1, D_FF), F32)
    yp, ckv_p, kr_p, cs_p, htail_p = _encoder_layer(
        x_prompt, 0, zeros_conv, zeros_ffn, None, None,
        mk_p.astype(BF16), mv_p.astype(BF16), wts, prm, cfg_p)
    ys, ckv_s, kr_s, cs_s, htail_s = _encoder_layer(
        x_sample, past, state_conv[l], state_ffn_conv[l], cache_ckv[l], cache_krope[l],
        cache_mem_k[l].reshape(bs, MEM_LEN, MEM_DIM).astype(BF16),
        cache_mem_v[l].reshape(bs, MEM_LEN, MEM_DIM).astype(BF16), wts, prm, cfg_s)

    tails = jnp.concatenate([htail_p.reshape(bp * 8, D_MODEL), htail_s.reshape(bs * 8, D_MODEL)], axis=0)
    a_tail = _norm_matmul(tails, prm["g_ffn"], wts["w_up"], tails.shape[0], 512, nout=D_FF)
    fs_p = a_tail[:bp * 8].reshape(bp, 8, D_FF)[:, 8 - (FFN_CONV_WIDTH - 1):]
    fs_s = a_tail[bp * 8:].reshape(bs, 8, D_FF)[:, 8 - (FFN_CONV_WIDTH - 1):]

    st = lambda a: a[None]
    return (yp, ys, st(ckv_p), st(kr_p), st(cs_p), st(fs_p),
            st(mk_p.reshape(bp, MEM_LEN, MEM_HEADS, MEM_HEAD_DIM)),
            st(mv_p.reshape(bp, MEM_LEN, MEM_HEADS, MEM_HEAD_DIM)),
            st(ckv_s), st(kr_s), st(cs_s), st(fs_s))
```
